```python
import jax, jax.numpy as jnp
from jax import lax
import numpy as np

D_MODEL = 1024
BATCH = 8
SEQ = 4096
DEPTH = 2
DEC_BATCH = 8
DEC_SEQ = 32
PAST_LEN = 2048

CHUNK = 64
Q_BLOCK = 128
EPS = 1e-6

POOL_GROUPS = 4
POOL_GROUP_DIM = 64
POOL_WIDTH = POOL_GROUPS * POOL_GROUP_DIM
POOL_WINDOWS = (2, 4, 8, 16)
POOL_HIST = 15

MLA_HEADS = 8
MLA_Q_LORA = 256
MLA_KV_LORA = 128
MLA_NOPE = 64
MLA_ROPE = 32
MLA_V = 64
MLA_QK = MLA_NOPE + MLA_ROPE
ROPE_THETA = 10000.0

SB_HEADS = 4
SB_HEAD_DIM = 64
SB_WIDTH = SB_HEADS * SB_HEAD_DIM

N_BRANCH = 3
IN_SPLITS = (POOL_WIDTH, MLA_Q_LORA, MLA_KV_LORA, MLA_ROPE, SB_WIDTH, SB_WIDTH, SB_WIDTH, N_BRANCH * D_MODEL)
IN_COLS = POOL_WIDTH + MLA_Q_LORA + MLA_KV_LORA + MLA_ROPE + 3 * SB_WIDTH + N_BRANCH * D_MODEL

N_GROUPS = 4
EXPERTS_PER_GROUP = 4
N_EXPERTS = N_GROUPS * EXPERTS_PER_GROUP
TOP_K = 2
D_EXPERT = 256

kernel_name = "hybrid_streaming_encoder_step"


def rmsnorm(x, g):
    x32 = x.astype(jnp.float32)
    ms = jnp.mean(x32 * x32, axis=-1, keepdims=True)
    return (x32 * lax.rsqrt(ms + EPS)).astype(x.dtype) * g


def rope(x, pos):
    half = x.shape[-1] // 2
    inv = ROPE_THETA ** (-jnp.arange(half, dtype=jnp.float32) / half)
    ang = pos.astype(jnp.float32)[:, None] * inv[None, :]
    cos = jnp.cos(ang)[None, :, None, :]
    sin = jnp.sin(ang)[None, :, None, :]
    x1 = x[..., :half].astype(jnp.float32)
    x2 = x[..., half:].astype(jnp.float32)
    return jnp.concatenate([x1 * cos - x2 * sin, x1 * sin + x2 * cos], axis=-1).astype(x.dtype)


def split_cols(z, sizes):
    out, start = [], 0
    for n in sizes:
        out.append(z[..., start:start + n])
        start += n
    return out


def pool_mix(u_ext, pos, w_pool_lin, pool_scale):
    B, Lx, _ = u_ext.shape
    L = Lx - POOL_HIST
    cs = jnp.cumsum(u_ext.astype(jnp.float32), axis=1)
    cs = jnp.pad(cs, ((0, 0), (1, 0), (0, 0)))
    u = u_ext[:, POOL_HIST:]
    end = cs[:, POOL_HIST + 1:POOL_HIST + 1 + L]
    outs = []
    for g, w in enumerate(POOL_WINDOWS):
        sl = slice(g * POOL_GROUP_DIM, (g + 1) * POOL_GROUP_DIM)
        start = cs[:, POOL_HIST + 1 - w:POOL_HIST + 1 - w + L, sl]
        cnt = jnp.minimum(pos + 1, w).astype(jnp.float32)[None, :, None]
        mean = (end[..., sl] - start) / cnt
        outs.append(mean.astype(u.dtype) - u[..., sl])
    y = jnp.stack(outs, axis=2)
    y = jnp.einsum('blgc,gcd->blgd', y, w_pool_lin).reshape(B, L, POOL_WIDTH)
    return y * pool_scale


def mla_keys(lat, kr, p):
    B, K = lat.shape[:2]
    kv = (lat @ p["w_ukv"]).reshape(B, K, MLA_HEADS, MLA_NOPE + MLA_V)
    k_nope, v = kv[..., :MLA_NOPE], kv[..., MLA_NOPE:]
    k_rope = jnp.broadcast_to(kr[:, :, None, :], (B, K, MLA_HEADS, MLA_ROPE))
    k = rmsnorm(jnp.concatenate([k_nope, k_rope], axis=-1), p["g_qk_k"])
    return k, v


def attend_mla(q, q_pos, k, v, k_pos):
    s = jnp.einsum('bqhd,bkhd->bhqk', q, k).astype(jnp.float32) * (MLA_QK ** -0.5)
    mask = (k_pos[None, :] // CHUNK) <= (q_pos[:, None] // CHUNK)
    s = jnp.where(mask[None, None], s, -jnp.inf)
    w = jax.nn.softmax(s, axis=-1).astype(v.dtype)
    return jnp.einsum('bhqk,bkhd->bqhd', w, v)


def attend_stick_breaking(q, q_pos, k, v, k_pos):
    z = jnp.einsum('bqhd,bkhd->bhqk', q, k).astype(jnp.float32) * (SB_HEAD_DIM ** -0.5)
    mask = (k_pos[None, :] < q_pos[:, None])[None, None]
    log_beta = jax.nn.log_sigmoid(z)
    log_keep = jnp.where(mask, jax.nn.log_sigmoid(-z), 0.0)
    between = lax.cumsum(log_keep, axis=3, reverse=True) - log_keep
    a = jnp.where(mask, jnp.exp(log_beta + between), 0.0).astype(v.dtype)
    return jnp.einsum('bhqk,bkhd->bqhd', a, v)


def sweep_queries(attend, q, q_pos, k, v, k_pos):
    B, L = q.shape[:2]
    if L <= Q_BLOCK:
        return attend(q, q_pos, k, v, k_pos)
    nb = L // Q_BLOCK
    qb = jnp.moveaxis(q.reshape(B, nb, Q_BLOCK, *q.shape[2:]), 1, 0)
    pb = q_pos.reshape(nb, Q_BLOCK)
    o = lax.map(lambda a: attend(a[0], a[1], k, v, k_pos), (qb, pb))
    return jnp.moveaxis(o, 0, 1).reshape(B, L, *o.shape[3:])


def token_mixers(xn, pos, p, hist):
    B, L, _ = xn.shape
    u_pool, c_q, c_kv, k_r, q_sb, k_sb, v_sb, gate_pre = split_cols(xn @ p["w_in"], IN_SPLITS)
    pool_hist = jnp.zeros((B, POOL_HIST, POOL_WIDTH), xn.dtype) if hist is None else hist["pool"]
    u_ext = jnp.concatenate([pool_hist, u_pool], axis=1)
    y_pool = pool_mix(u_ext, pos, p["w_pool_lin"], p["pool_scale"])
    q = (rmsnorm(c_q, p["g_q_lora"]) @ p["w_uq"]).reshape(B, L, MLA_HEADS, MLA_QK)
    q = jnp.concatenate([q[..., :MLA_NOPE], rope(q[..., MLA_NOPE:], pos)], axis=-1)
    q = rmsnorm(q, p["g_qk_q"])
    lat_new = rmsnorm(c_kv, p["g_kv_lora"])
    kr_new = rope(k_r[:, :, None, :], pos)[:, :, 0, :]
    q_sb = q_sb.reshape(B, L, SB_HEADS, SB_HEAD_DIM)
    k_sb = k_sb.reshape(B, L, SB_HEADS, SB_HEAD_DIM)
    v_sb = v_sb.reshape(B, L, SB_HEADS, SB_HEAD_DIM)
    if hist is None:
        lat, kr, ksb, vsb = lat_new, kr_new, k_sb, v_sb
    else:
        lat = jnp.concatenate([hist["lat"], lat_new], axis=1)
        kr = jnp.concatenate([hist["kr"], kr_new], axis=1)
        ksb = jnp.concatenate([hist["sbk"], k_sb], axis=1)
        vsb = jnp.concatenate([hist["sbv"], v_sb], axis=1)
    k_pos = jnp.arange(lat.shape[1], dtype=jnp.int32)
    k_mla, v_mla = mla_keys(lat, kr, p)
    y_mla = sweep_queries(attend_mla, q, pos, k_mla, v_mla, k_pos).reshape(B, L, MLA_HEADS * MLA_V)
    y_sb = sweep_queries(attend_stick_breaking, q_sb, pos, ksb, vsb, k_pos).reshape(B, L, SB_WIDTH)
    g = jax.nn.sigmoid(gate_pre).reshape(B, L, N_BRANCH, D_MODEL)
    h = (g[:, :, 0] * (y_pool @ p["w_br_pool"])
         + g[:, :, 1] * (y_mla @ p["w_br_mla"])
         + g[:, :, 2] * (y_sb @ p["w_br_sb"]))
    state = (lat_new, kr_new, k_sb, v_sb, u_ext[:, -POOL_HIST:])
    return h @ p["w_out"], state


def hier_moe(xn, p):
    B, L, _ = xn.shape
    lg = (xn @ p["w_router_group"]).astype(jnp.float32) + p["b_router_group"]
    pg = jax.nn.softmax(lg, axis=-1)
    grp = jnp.argmax(lg, axis=-1)
    p_grp = jnp.take_along_axis(pg, grp[..., None], axis=-1)
    le = ((xn @ p["w_router_expert"]).astype(jnp.float32) + p["b_router_expert"]).reshape(B, L, N_GROUPS, EXPERTS_PER_GROUP)
    le_g = jnp.take_along_axis(le, grp[..., None, None], axis=2)[:, :, 0, :]
    top_v, top_i = lax.top_k(le_g, TOP_K)
    w_top = jax.nn.softmax(top_v, axis=-1) * p_grp
    expert_id = grp[..., None] * EXPERTS_PER_GROUP + top_i
    combine = jnp.sum(jax.nn.one_hot(expert_id, N_EXPERTS, dtype=jnp.float32) * w_top[..., None], axis=-2)
    combine = combine.astype(xn.dtype)
    out = jnp.zeros_like(xn)
    for e in range(N_EXPERTS):
        h = jax.nn.silu(xn @ p["w_exp_gate"][e]) * (xn @ p["w_exp_up"][e])
        out = out + combine[..., e:e + 1] * (h @ p["w_exp_down"][e])
    return out


def trunk_layer(x, pos, p, hist):
    mix, state = token_mixers(rmsnorm(x, p["g_mix_norm"]), pos, p, hist)
    x = x + mix
    x = x + hier_moe(rmsnorm(x, p["g_ffn_norm"]), p)
    return x, state


def setup_inputs(seed: int = 0) -> dict:
    key = jax.random.key(seed)
    ks = iter(jax.random.split(key, 32))
    f32 = jnp.float32

    def nrm(shape, scale):
        return jax.random.normal(next(ks), shape, f32) * scale

    def gain(shape):
        return 1.0 + 0.05 * jax.random.normal(next(ks), shape, f32)

    return {
        "x_prompt": nrm((BATCH, SEQ, D_MODEL), 1.0),
        "x_sample": nrm((DEC_BATCH, DEC_SEQ, D_MODEL), 1.0),
        "cache_mla_latent": nrm((DEPTH, DEC_BATCH, PAST_LEN, MLA_KV_LORA), 1.0),
        "cache_mla_krope": nrm((DEPTH, DEC_BATCH, PAST_LEN, MLA_ROPE), 1.0),
        "cache_sb_k": nrm((DEPTH, DEC_BATCH, PAST_LEN, SB_HEADS, SB_HEAD_DIM), 1.0),
        "cache_sb_v": nrm((DEPTH, DEC_BATCH, PAST_LEN, SB_HEADS, SB_HEAD_DIM), 1.0),
        "state_pool": nrm((DEPTH, DEC_BATCH, POOL_HIST, POOL_WIDTH), 1.0),
        "g_mix_norm": gain((DEPTH, D_MODEL)),
        "w_in": nrm((DEPTH, D_MODEL, IN_COLS), D_MODEL ** -0.5),
        "w_pool_lin": nrm((DEPTH, POOL_GROUPS, POOL_GROUP_DIM, POOL_GROUP_DIM), POOL_GROUP_DIM ** -0.5),
        "pool_scale": gain((DEPTH, POOL_WIDTH)),
        "g_q_lora": gain((DEPTH, MLA_Q_LORA)),
        "w_uq": nrm((DEPTH, MLA_Q_LORA, MLA_HEADS * MLA_QK), MLA_Q_LORA ** -0.5),
        "g_kv_lora": gain((DEPTH, MLA_KV_LORA)),
        "w_ukv": nrm((DEPTH, MLA_KV_LORA, MLA_HEADS * (MLA_NOPE + MLA_V)), MLA_KV_LORA ** -0.5),
        "g_qk_q": gain((DEPTH, MLA_QK)),
        "g_qk_k": gain((DEPTH, MLA_QK)),
        "w_br_pool": nrm((DEPTH, POOL_WIDTH, D_MODEL), POOL_WIDTH ** -0.5),
        "w_br_mla": nrm((DEPTH, MLA_HEADS * MLA_V, D_MODEL), (MLA_HEADS * MLA_V) ** -0.5),
        "w_br_sb": nrm((DEPTH, SB_WIDTH, D_MODEL), SB_WIDTH ** -0.5),
        "w_out": nrm((DEPTH, D_MODEL, D_MODEL), D_MODEL ** -0.5),
        "g_ffn_norm": gain((DEPTH, D_MODEL)),
        "w_router_group": nrm((DEPTH, D_MODEL, N_GROUPS), D_MODEL ** -0.5),
        "b_router_group": nrm((DEPTH, N_GROUPS), 0.01),
        "w_router_expert": nrm((DEPTH, D_MODEL, N_EXPERTS), D_MODEL ** -0.5),
        "b_router_expert": nrm((DEPTH, N_EXPERTS), 0.01),
        "w_exp_gate": nrm((DEPTH, N_EXPERTS, D_MODEL, D_EXPERT), D_MODEL ** -0.5),
        "w_exp_up": nrm((DEPTH, N_EXPERTS, D_MODEL, D_EXPERT), D_MODEL ** -0.5),
        "w_exp_down": nrm((DEPTH, N_EXPERTS, D_EXPERT, D_MODEL), D_EXPERT ** -0.5),
    }


def reference(x_prompt, x_sample, cache_mla_latent, cache_mla_krope, cache_sb_k, cache_sb_v, state_pool,
              g_mix_norm, w_in, w_pool_lin, pool_scale, g_q_lora, w_uq, g_kv_lora, w_ukv, g_qk_q, g_qk_k,
              w_br_pool, w_br_mla, w_br_sb, w_out, g_ffn_norm, w_router_group, b_router_group,
              w_router_expert, b_router_expert, w_exp_gate, w_exp_up, w_exp_down):
    past = cache_mla_latent.shape[2]
    pos_p = jnp.arange(x_prompt.shape[1], dtype=jnp.int32)
    pos_s = past + jnp.arange(x_sample.shape[1], dtype=jnp.int32)
    xp, xs = x_prompt, x_sample
    new_p = [[], [], [], [], []]
    new_s = [[], [], [], [], []]
    for l in range(DEPTH):
        p = {
            "g_mix_norm": g_mix_norm[l], "w_in": w_in[l], "w_pool_lin": w_pool_lin[l],
            "pool_scale": pool_scale[l], "g_q_lora": g_q_lora[l], "w_uq": w_uq[l],
            "g_kv_lora": g_kv_lora[l], "w_ukv": w_ukv[l], "g_qk_q": g_qk_q[l], "g_qk_k": g_qk_k[l],
            "w_br_pool": w_br_pool[l], "w_br_mla": w_br_mla[l], "w_br_sb": w_br_sb[l],
            "w_out": w_out[l], "g_ffn_norm": g_ffn_norm[l],
            "w_router_group": w_router_group[l], "b_router_group": b_router_group[l],
            "w_router_expert": w_router_expert[l], "b_router_expert": b_router_expert[l],
            "w_exp_gate": w_exp_gate[l], "w_exp_up": w_exp_up[l], "w_exp_down": w_exp_down[l],
        }
        hist = {"lat": cache_mla_latent[l], "kr": cache_mla_krope[l], "sbk": cache_sb_k[l],
                "sbv": cache_sb_v[l], "pool": state_pool[l]}
        xp, st_p = trunk_layer(xp, pos_p, p, None)
        xs, st_s = trunk_layer(xs, pos_s, p, hist)
        for i in range(5):
            new_p[i].append(st_p[i])
            new_s[i].append(st_s[i])
    return (xp, xs,
            jnp.stack(new_p[0]), jnp.stack(new_p[1]), jnp.stack(new_p[2]), jnp.stack(new_p[3]), jnp.stack(new_p[4]),
            jnp.stack(new_s[0]), jnp.stack(new_s[1]), jnp.stack(new_s[2]), jnp.stack(new_s[3]), jnp.stack(new_s[4]))
```

```python
import functools

import jax
import jax.numpy as jnp
from jax import lax
from jax.experimental import pallas as pl
from jax.experimental.pallas import tpu as pltpu

F32 = jnp.float32
BF16 = jnp.bfloat16

D_MODEL = 1024
CHUNK_SHIFT = 6
EPS = 1e-6
POOL_WIDTH = 256
POOL_HIST = 15
POOL_HALO = 16
MLA_HEADS = 8
MLA_Q_LORA = 256
MLA_KV_LORA = 128
MLA_NOPE = 64
MLA_ROPE = 32
MLA_V = 64
MLA_QK = MLA_NOPE + MLA_ROPE
ROPE_THETA = 10000.0
SB_HEADS = 4
SB_HEAD_DIM = 64
SB_WIDTH = 256
N_GROUPS = 4
EXPERTS_PER_GROUP = 4
N_EXPERTS = 16
D_EXPERT = 256
LANES = 128
ROUTER_EXPERT_LANE0 = N_GROUPS
VMEM_LIMIT = 56 * 1024 * 1024

C_UPOOL = 0
C_CQ = 256
C_CKV = 512
C_QSB = 640
C_KSB = 896
C_VSB = 1152
C_KRA = 1408
C_KRB = 1536
C_TOTAL = 1664


def _tile(n, pref, mult=16):
    best = None
    for t in range(mult, min(n, pref) + 1, mult):
        if n % t == 0:
            best = t
    return best if best is not None else n


def _params(*sem):
    return pltpu.CompilerParams(dimension_semantics=sem, vmem_limit_bytes=VMEM_LIMIT)


def _rms(x, inv_n=None):
    if inv_n is None:
        ms = jnp.mean(x * x, axis=-1, keepdims=True)
    else:
        ms = jnp.sum(x * x, axis=-1, keepdims=True) * inv_n
    return x * lax.rsqrt(ms + EPS)


def _dot(a, b):
    return jnp.dot(a, b, preferred_element_type=F32)


def _dot_nt(a, b):
    return lax.dot_general(a, b, (((1,), (1,)), ((), ())), preferred_element_type=F32)


def _inproj_kernel(x_ref, g_ref, w_ref, cs_ref, gq_ref, wq_ref, wqs_ref, gqk_ref, gkv_ref,
                   up_ref, q_ref, lat_ref, kr_ref, qsb_ref, ksb_ref, vsb_ref, ksbh_ref, vsbh_ref):
    x = x_ref[...]
    xn = (_rms(x) * g_ref[...]).astype(BF16)
    z = _dot(xn, w_ref[...])
    up_ref[...] = z[:, C_UPOOL:C_UPOOL + POOL_WIDTH]
    cq = z[:, C_CQ:C_CQ + MLA_Q_LORA]
    cqn = (_rms(cq) * gq_ref[...]).astype(BF16)
    qa = _dot(cqn, wq_ref[...])
    qb = _dot(cqn, wqs_ref[...])
    cos = cs_ref[0]
    sin = cs_ref[1]
    gqk = gqk_ref[...]
    for h in range(MLA_HEADS):
        sl = slice(h * LANES, (h + 1) * LANES)
        qh = qa[:, sl] * cos + qb[:, sl] * sin
        q_ref[h] = (_rms(qh, 1.0 / MLA_QK) * gqk).astype(BF16)
    ckv = z[:, C_CKV:C_CKV + MLA_KV_LORA]
    lat_ref[...] = _rms(ckv) * gkv_ref[...]
    kr_ref[...] = z[:, C_KRA:C_KRA + LANES] * cos + z[:, C_KRB:C_KRB + LANES] * sin
    qsb_ref[...] = (z[:, C_QSB:C_QSB + SB_WIDTH] * (SB_HEAD_DIM ** -0.5)).astype(BF16)
    ksb = z[:, C_KSB:C_KSB + SB_WIDTH]
    ksb_ref[...] = ksb
    ksbh_ref[...] = ksb.astype(BF16)
    vsb = z[:, C_VSB:C_VSB + SB_WIDTH]
    vsb_ref[...] = vsb
    vsbh_ref[...] = vsb.astype(BF16)


def _inproj(x2, cs, lw, seq_len):
    t = x2.shape[0]
    tm = _tile(t, 512)
    if tm <= seq_len and seq_len % tm == 0:
        per = seq_len // tm
        cs_map = lambda i: (0, i % per, 0)
    else:
        assert tm % seq_len == 0
        cs = jnp.tile(cs, (1, tm // seq_len, 1))
        cs_map = lambda i: (0, 0, 0)
    row = lambda i: (i, 0)
    const2 = lambda i: (0, 0)
    out_shape = (
        jax.ShapeDtypeStruct((t, POOL_WIDTH), F32),
        jax.ShapeDtypeStruct((MLA_HEADS, t, LANES), BF16),
        jax.ShapeDtypeStruct((t, MLA_KV_LORA), F32),
        jax.ShapeDtypeStruct((t, LANES), F32),
        jax.ShapeDtypeStruct((t, SB_WIDTH), BF16),
        jax.ShapeDtypeStruct((t, SB_WIDTH), F32),
        jax.ShapeDtypeStruct((t, SB_WIDTH), F32),
        jax.ShapeDtypeStruct((t, SB_WIDTH), BF16),
        jax.ShapeDtypeStruct((t, SB_WIDTH), BF16),
    )
    out_specs = (
        pl.BlockSpec((tm, POOL_WIDTH), row),
        pl.BlockSpec((MLA_HEADS, tm, LANES), lambda i: (0, i, 0)),
        pl.BlockSpec((tm, MLA_KV_LORA), row),
        pl.BlockSpec((tm, LANES), row),
        pl.BlockSpec((tm, SB_WIDTH), row),
        pl.BlockSpec((tm, SB_WIDTH), row),
        pl.BlockSpec((tm, SB_WIDTH), row),
        pl.BlockSpec((tm, SB_WIDTH), row),
        pl.BlockSpec((tm, SB_WIDTH), row),
    )
    in_specs = [
        pl.BlockSpec((tm, D_MODEL), row),
        pl.BlockSpec((1, D_MODEL), const2),
        pl.BlockSpec((D_MODEL, C_TOTAL), const2),
        pl.BlockSpec((2, tm, LANES), cs_map),
        pl.BlockSpec((1, MLA_Q_LORA), const2),
        pl.BlockSpec((MLA_Q_LORA, MLA_HEADS * LANES), const2),
        pl.BlockSpec((MLA_Q_LORA, MLA_HEADS * LANES), const2),
        pl.BlockSpec((1, LANES), const2),
        pl.BlockSpec((1, MLA_KV_LORA), const2),
    ]
    return pl.pallas_call(
        _inproj_kernel, grid=(t // tm,), in_specs=in_specs, out_specs=out_specs, out_shape=out_shape,
        compiler_params=_params("parallel"), name="inproj",
    )(x2, lw["g_mix"], lw["w_cat"], cs, lw["g_q_lora"], lw["wq"], lw["wqs"], lw["g_qk_q"], lw["g_kv"])


def _mla_keys_kernel(lat_ref, kr_ref, wk_ref, wv_ref, g_ref, k_ref, v_ref):
    latb = lat_ref[...].astype(BF16)
    kn = _dot(latb, wk_ref[...])
    vv = _dot(latb, wv_ref[...])
    kr = kr_ref[...]
    g = g_ref[...]
    for h in range(MLA_HEADS):
        sl = slice(h * LANES, (h + 1) * LANES)
        kh = kn[:, sl] + kr
        k_ref[h] = (_rms(kh, 1.0 / MLA_QK) * g).astype(BF16)
        v_ref[h] = vv[:, sl].astype(BF16)


def _mla_keys(lat2, kr2, lw):
    t = lat2.shape[0]
    tm = _tile(t, 1024)
    row = lambda i: (i, 0)
    const2 = lambda i: (0, 0)
    hd = lambda i: (0, i, 0)
    shp = jax.ShapeDtypeStruct((MLA_HEADS, t, LANES), BF16)
    return pl.pallas_call(
        _mla_keys_kernel, grid=(t // tm,),
        in_specs=[pl.BlockSpec((tm, MLA_KV_LORA), row), pl.BlockSpec((tm, LANES), row),
                  pl.BlockSpec((MLA_KV_LORA, MLA_HEADS * LANES), const2),
                  pl.BlockSpec((MLA_KV_LORA, MLA_HEADS * LANES), const2),
                  pl.BlockSpec((1, LANES), const2)],
        out_specs=(pl.BlockSpec((MLA_HEADS, tm, LANES), hd), pl.BlockSpec((MLA_HEADS, tm, LANES), hd)),
        out_shape=(shp, shp), compiler_params=_params("parallel"), name="mla_keys",
    )(lat2, kr2, lw["wk"], lw["wv"], lw["g_qk_k"])


def _mla_attn_kernel(q_ref, k_ref, v_ref, o_ref, m_ref, l_ref, acc_ref, *, tq, tk, q_off, kv_len):
    i = pl.program_id(2)
    q0 = q_off + i * tq
    vis_all = jnp.minimum(((q0 >> CHUNK_SHIFT) + 1) << CHUNK_SHIFT, kv_len)
    vis_any = jnp.minimum((((q0 + tq - 1) >> CHUNK_SHIFT) + 1) << CHUNK_SHIFT, kv_len)
    n_full = vis_all // tk
    n_blk = (vis_any + tk - 1) // tk
    scale = MLA_QK ** -0.5
    for hh in range(2):
        q = q_ref[hh]
        m_ref[...] = jnp.full((tq, 1), -jnp.inf, F32)
        l_ref[...] = jnp.zeros((tq, 1), F32)
        acc_ref[hh] = jnp.zeros((tq, LANES), F32)

        def step(j, masked, hh=hh, q=q):
            start = pl.multiple_of(j * tk, tk)
            k = k_ref[hh, pl.ds(start, tk), :]
            s = _dot_nt(q, k) * scale
            if masked:
                qpos = q0 + lax.broadcasted_iota(jnp.int32, (tq, tk), 0)
                kpos = start + lax.broadcasted_iota(jnp.int32, (tq, tk), 1)
                ok = ((kpos >> CHUNK_SHIFT) <= (qpos >> CHUNK_SHIFT)) & (kpos < kv_len)
                s = jnp.where(ok, s, -jnp.inf)
            m_prev = m_ref[...]
            m_new = jnp.maximum(m_prev, jnp.max(s, axis=1, keepdims=True))
            alpha = jnp.exp(m_prev - m_new)
            p = jnp.exp(s - m_new)
            l_ref[...] = alpha * l_ref[...] + jnp.sum(p, axis=1, keepdims=True)
            v = v_ref[hh, pl.ds(start, tk), :]
            acc_ref[hh] = alpha * acc_ref[hh] + _dot(p.astype(BF16), v)
            m_ref[...] = m_new

        def full_body(j, c):
            step(j, False)
            return c

        def masked_body(j, c):
            step(j, True)
            return c

        lax.fori_loop(0, n_full, full_body, 0)
        lax.fori_loop(n_full, n_blk, masked_body, 0)
        acc_ref[hh] = acc_ref[hh] / l_ref[...]
    o_ref[...] = (acc_ref[0] + acc_ref[1]).astype(BF16)


def _mla_attn(q, k, v, batch, lq, lk, q_off, kv_len):
    tq = _tile(lq, 256)
    tk = _tile(lk, 256)
    nq = lq // tq
    kern = functools.partial(_mla_attn_kernel, tq=tq, tk=tk, q_off=q_off, kv_len=kv_len)
    return pl.pallas_call(
        kern, grid=(batch, MLA_HEADS // 2, nq),
        in_specs=[pl.BlockSpec((2, tq, LANES), lambda b, p, i: (p, b * nq + i, 0)),
                  pl.BlockSpec((2, lk, LANES), lambda b, p, i: (p, b, 0)),
                  pl.BlockSpec((2, lk, LANES), lambda b, p, i: (p, b, 0))],
        out_specs=pl.BlockSpec((tq, LANES), lambda b, p, i: (b * nq + i, p)),
        out_shape=jax.ShapeDtypeStruct((batch * lq, MLA_HEADS * MLA_V), BF16),
        scratch_shapes=[pltpu.VMEM((tq, 1), F32), pltpu.VMEM((tq, 1), F32),
                        pltpu.VMEM((2, tq, LANES), F32)],
        compiler_params=_params("parallel", "parallel", "arbitrary"), name="mla_attn",
    )(q, k, v)


def _sb_attn_kernel(q_ref, k_ref, v_ref, o_ref, c_ref, acc_ref, *, tq, tk, q_off):
    i = pl.program_id(2)
    q0 = q_off + i * tq
    n_full = q0 // tk
    n_blk = (q0 + tq - 2) // tk + 1
    lane = lax.broadcasted_iota(jnp.int32, (1, LANES), 1)
    upper = (lax.broadcasted_iota(jnp.int32, (tk, tk), 0) >
             lax.broadcasted_iota(jnp.int32, (tk, tk), 1)).astype(BF16)
    acc_ref[...] = jnp.zeros((tq, LANES), F32)
    for hh in range(2):
        in_head = (lane >= hh * SB_HEAD_DIM) & (lane < (hh + 1) * SB_HEAD_DIM)
        q = jnp.where(in_head, q_ref[...], jnp.zeros((), BF16))
        c_ref[...] = jnp.zeros((tq, 1), F32)

        def step(j, masked, q=q, in_head=in_head):
            start = pl.multiple_of(j * tk, tk)
            k = k_ref[pl.ds(start, tk), :]
            z = _dot_nt(q, k)
            t = jnp.log1p(jnp.exp(-jnp.abs(z)))
            log_beta = -(jnp.maximum(-z, 0.0) + t)
            log_keep = -(jnp.maximum(z, 0.0) + t)
            if masked:
                qpos = q0 + lax.broadcasted_iota(jnp.int32, (tq, tk), 0)
                kpos = start + lax.broadcasted_iota(jnp.int32, (tq, tk), 1)
                ok = kpos < qpos
                log_keep = jnp.where(ok, log_keep, 0.0)
            hi = log_keep.astype(BF16)
            lo = (log_keep - hi.astype(F32)).astype(BF16)
            between = _dot(hi, upper) + _dot(lo, upper) + c_ref[...]
            a = jnp.exp(log_beta + between)
            if masked:
                a = jnp.where(ok, a, 0.0)
            v = jnp.where(in_head, v_ref[pl.ds(start, tk), :], jnp.zeros((), BF16))
            acc_ref[...] += _dot(a.astype(BF16), v)
            c_ref[...] += jnp.sum(log_keep, axis=1, keepdims=True)

        def masked_body(it, c):
            step(n_blk - 1 - it, True)
            return c

        def full_body(it, c):
            step(n_full - 1 - it, False)
            return c

        lax.fori_loop(0, n_blk - n_full, masked_body, 0)
        lax.fori_loop(0, n_full, full_body, 0)
    o_ref[...] = acc_ref[...].astype(BF16)


def _sb_attn(q, k, v, batch, lq, lk, q_off):
    tq = _tile(lq, 256)
    tk = _tile(lk, 256)
    nq = lq // tq
    kern = functools.partial(_sb_attn_kernel, tq=tq, tk=tk, q_off=q_off)
    return pl.pallas_call(
        kern, grid=(batch, SB_HEADS // 2, nq),
        in_specs=[pl.BlockSpec((tq, LANES), lambda b, p, i: (b * nq + i, p)),
                  pl.BlockSpec((lk, LANES), lambda b, p, i: (b, p)),
                  pl.BlockSpec((lk, LANES), lambda b, p, i: (b, p))],
        out_specs=pl.BlockSpec((tq, LANES), lambda b, p, i: (b * nq + i, p)),
        out_shape=jax.ShapeDtypeStruct((batch * lq, SB_WIDTH), BF16),
        scratch_shapes=[pltpu.VMEM((tq, 1), F32), pltpu.VMEM((tq, LANES), F32)],
        compiler_params=_params("parallel", "parallel", "arbitrary"), name="sb_attn",
    )(q, k, v)


def _pool_kernel(u_ref, halo_ref, hist_ref, w_ref, sc_ref, y_ref, xb_ref, *, tm, past):
    i = pl.program_id(1)
    xb_ref[POOL_HALO:POOL_HALO + tm, :] = u_ref[...]

    @pl.when(i == 0)
    def _():
        xb_ref[0:POOL_HALO, :] = hist_ref[0]

    @pl.when(i > 0)
    def _():
        xb_ref[0:POOL_HALO, :] = halo_ref[...]

    def back(d):
        return xb_ref[POOL_HALO - d:POOL_HALO - d + tm, :]

    x0 = back(0)
    a2 = x0 + back(1)
    a4 = a2 + back(2) + back(3)
    a8 = a4 + back(4) + back(5) + back(6) + back(7)
    a16 = a8
    for d in range(8, 16):
        a16 = a16 + back(d)
    grp = lax.broadcasted_iota(jnp.int32, (1, POOL_WIDTH), 1) >> 6
    win = jnp.where(grp == 0, a2, jnp.where(grp == 1, a4, jnp.where(grp == 2, a8, a16)))
    width = jnp.where(grp == 0, 2.0, jnp.where(grp == 1, 4.0, jnp.where(grp == 2, 8.0, 16.0)))
    pos = past + i * tm + lax.broadcasted_iota(jnp.int32, (tm, 1), 0)
    cnt = jnp.minimum((pos + 1).astype(F32), width)
    d = win / cnt - x0
    y_ref[...] = (_dot(d.astype(BF16), w_ref[...]) * sc_ref[...]).astype(BF16)


def _pool(u2, hist16, lw, batch, seq_len, past):
    tm = _tile(seq_len, 512)
    nl = seq_len // tm
    per = tm // POOL_HALO
    kern = functools.partial(_pool_kernel, tm=tm, past=past)
    return pl.pallas_call(
        kern, grid=(batch, nl),
        in_specs=[pl.BlockSpec((tm, POOL_WIDTH), lambda b, i: (b * nl + i, 0)),
                  pl.BlockSpec((POOL_HALO, POOL_WIDTH),
                               lambda b, i: (jnp.maximum((b * nl + i) * per - 1, 0), 0)),
                  pl.BlockSpec((1, POOL_HALO, POOL_WIDTH), lambda b, i: (b, 0, 0)),
                  pl.BlockSpec((POOL_WIDTH, POOL_WIDTH), lambda b, i: (0, 0)),
                  pl.BlockSpec((1, POOL_WIDTH), lambda b, i: (0, 0))],
        out_specs=pl.BlockSpec((tm, POOL_WIDTH), lambda b, i: (b * nl + i, 0)),
        out_shape=jax.ShapeDtypeStruct((batch * seq_len, POOL_WIDTH), BF16),
        scratch_shapes=[pltpu.VMEM((POOL_HALO + tm, POOL_WIDTH), F32)],
        compiler_params=_params("parallel", "arbitrary"), name="pool",
    )(u2, u2, hist16, lw["w_pool"], lw["pool_scale"])


def _sigmoid(x):
    return 1.0 / (1.0 + jnp.exp(-x))


def _merge_kernel(x_ref, yp_ref, ym_ref, ys_ref, g1_ref, wg_ref, wbp_ref, wbm_ref, wbs_ref, wo_ref,
                  g2_ref, wr_ref, br_ref, x1_ref, xn2_ref, comb_ref):
    x = x_ref[...]
    xn = (_rms(x) * g1_ref[...]).astype(BF16)
    h = _sigmoid(_dot(xn, wg_ref[:, 0:D_MODEL])) * _dot(yp_ref[...], wbp_ref[...])
    h = h + _sigmoid(_dot(xn, wg_ref[:, D_MODEL:2 * D_MODEL])) * _dot(ym_ref[...], wbm_ref[...])
    h = h + _sigmoid(_dot(xn, wg_ref[:, 2 * D_MODEL:3 * D_MODEL])) * _dot(ys_ref[...], wbs_ref[...])
    x1 = x + _dot(h.astype(BF16), wo_ref[...])
    x1_ref[...] = x1
    xn2 = (_rms(x1) * g2_ref[...]).astype(BF16)
    xn2_ref[...] = xn2
    lg = _dot(xn2, wr_ref[...]) + br_ref[...]
    lane = lax.broadcasted_iota(jnp.int32, lg.shape, 1)
    lanef = lane.astype(F32)
    neg = -jnp.inf
    big = 1e9
    is_g = lane < N_GROUPS
    glog = jnp.where(is_g, lg, neg)
    gmax = jnp.max(glog, axis=1, keepdims=True)
    grp = jnp.min(jnp.where(glog == gmax, lanef, big), axis=1, keepdims=True)
    p_grp = 1.0 / jnp.sum(jnp.where(is_g, jnp.exp(lg - gmax), 0.0), axis=1, keepdims=True)
    eid = lanef - float(ROUTER_EXPERT_LANE0)
    lo = grp * float(EXPERTS_PER_GROUP)
    in_g = (eid >= lo) & (eid < lo + float(EXPERTS_PER_GROUP))
    e1 = jnp.where(in_g, lg, neg)
    v1 = jnp.max(e1, axis=1, keepdims=True)
    i1 = jnp.min(jnp.where(in_g & (e1 == v1), eid, big), axis=1, keepdims=True)
    rest = in_g & (eid != i1)
    e2 = jnp.where(rest, lg, neg)
    v2 = jnp.max(e2, axis=1, keepdims=True)
    i2 = jnp.min(jnp.where(rest & (e2 == v2), eid, big), axis=1, keepdims=True)
    ex = jnp.exp(v2 - v1)
    w1 = (1.0 / (1.0 + ex)) * p_grp
    w2 = (ex / (1.0 + ex)) * p_grp
    comb_ref[...] = jnp.where(eid == i1, w1, 0.0) + jnp.where(eid == i2, w2, 0.0)


def _merge(x2, yp, ym, ys, lw):
    t = x2.shape[0]
    tm = _tile(t, 256)
    row = lambda i: (i, 0)
    const2 = lambda i: (0, 0)
    return pl.pallas_call(
        _merge_kernel, grid=(t // tm,),
        in_specs=[pl.BlockSpec((tm, D_MODEL), row),
                  pl.BlockSpec((tm, POOL_WIDTH), row),
                  pl.BlockSpec((tm, MLA_HEADS * MLA_V), row),
                  pl.BlockSpec((tm, SB_WIDTH), row),
                  pl.BlockSpec((1, D_MODEL), const2),
                  pl.BlockSpec((D_MODEL, 3 * D_MODEL), const2),
                  pl.BlockSpec((POOL_WIDTH, D_MODEL), const2),
                  pl.BlockSpec((MLA_HEADS * MLA_V, D_MODEL), const2),
                  pl.BlockSpec((SB_WIDTH, D_MODEL), const2),
                  pl.BlockSpec((D_MODEL, D_MODEL), const2),
                  pl.BlockSpec((1, D_MODEL), const2),
                  pl.BlockSpec((D_MODEL, LANES), const2),
                  pl.BlockSpec((1, LANES), const2)],
        out_specs=(pl.BlockSpec((tm, D_MODEL), row), pl.BlockSpec((tm, D_MODEL), row),
                   pl.BlockSpec((tm, LANES), row)),
        out_shape=(jax.ShapeDtypeStruct((t, D_MODEL), F32), jax.ShapeDtypeStruct((t, D_MODEL), BF16),
                   jax.ShapeDtypeStruct((t, LANES), F32)),
        compiler_params=_params("parallel"), name="merge",
    )(x2, yp, ym, ys, lw["g_mix"], lw["w_gate"], lw["w_br_pool"], lw["w_br_mla"], lw["w_br_sb"],
      lw["w_out"], lw["g_ffn"], lw["w_router"], lw["b_router"])


def _moe_kernel(xn_ref, comb_ref, x1_ref, wg_ref, wu_ref, wd_ref, o_ref, acc_ref):
    e = pl.program_id(1)

    @pl.when(e == 0)
    def _():
        acc_ref[...] = jnp.zeros(acc_ref.shape, F32)

    xn = xn_ref[...]
    a = _dot(xn, wg_ref[0])
    h = (a * _sigmoid(a)) * _dot(xn, wu_ref[0])
    y = _dot(h.astype(BF16), wd_ref[0])
    comb = comb_ref[...]
    lane = lax.broadcasted_iota(jnp.int32, comb.shape, 1)
    c = jnp.sum(jnp.where(lane == e + ROUTER_EXPERT_LANE0, comb, 0.0), axis=1, keepdims=True)
    acc_ref[...] += c * y

    @pl.when(e == N_EXPERTS - 1)
    def _():
        o_ref[...] = x1_ref[...] + acc_ref[...]


def _moe(xn2, comb, x1, lw):
    t = xn2.shape[0]
    tm = _tile(t, 1024)
    row = lambda i, e: (i, 0)
    return pl.pallas_call(
        _moe_kernel, grid=(t // tm, N_EXPERTS),
        in_specs=[pl.BlockSpec((tm, D_MODEL), row), pl.BlockSpec((tm, LANES), row),
                  pl.BlockSpec((tm, D_MODEL), row),
                  pl.BlockSpec((1, D_MODEL, D_EXPERT), lambda i, e: (e, 0, 0)),
                  pl.BlockSpec((1, D_MODEL, D_EXPERT), lambda i, e: (e, 0, 0)),
                  pl.BlockSpec((1, D_EXPERT, D_MODEL), lambda i, e: (e, 0, 0))],
        out_specs=pl.BlockSpec((tm, D_MODEL), row),
        out_shape=jax.ShapeDtypeStruct((t, D_MODEL), F32),
        scratch_shapes=[pltpu.VMEM((tm, D_MODEL), F32)],
        compiler_params=_params("parallel", "arbitrary"), name="moe",
    )(xn2, comb, x1, lw["w_exp_gate"], lw["w_exp_up"], lw["w_exp_down"])


def _head_blocks(w, n_heads, width, place):
    k = w.shape[0]
    blocks = [place(h, w[:, h * width:(h + 1) * width]) for h in range(n_heads)]
    return jnp.concatenate(blocks, axis=1).reshape(k, n_heads * LANES)


def _pad_lanes(w, before, total=LANES):
    return jnp.pad(w, ((0, 0), (before, total - before - w.shape[1])))


def _swap_halves(w):
    half = w.shape[1] // 2
    return jnp.concatenate([-w[:, half:], w[:, :half]], axis=1)


def _prep_layer(p):
    w_in = p["w_in"]
    w_kr = w_in[:, 640:672]
    w_cat = jnp.concatenate(
        [w_in[:, 0:640], w_in[:, 672:1440],
         _pad_lanes(w_kr, MLA_NOPE), _pad_lanes(_swap_halves(w_kr), MLA_NOPE)], axis=1)
    wq = _head_blocks(p["w_uq"], MLA_HEADS, MLA_QK, lambda h, b: _pad_lanes(b, 0))
    wqs = _head_blocks(p["w_uq"], MLA_HEADS, MLA_QK,
                       lambda h, b: _pad_lanes(_swap_halves(b[:, MLA_NOPE:]), MLA_NOPE))
    wk = _head_blocks(p["w_ukv"], MLA_HEADS, MLA_NOPE + MLA_V,
                      lambda h, b: _pad_lanes(b[:, :MLA_NOPE], 0))
    wv = _head_blocks(p["w_ukv"], MLA_HEADS, MLA_NOPE + MLA_V,
                      lambda h, b: _pad_lanes(b[:, MLA_NOPE:], (h % 2) * MLA_V))
    w_pool = jax.scipy.linalg.block_diag(*[p["w_pool_lin"][g] for g in range(4)])
    w_router = _pad_lanes(jnp.concatenate([p["w_router_group"], p["w_router_expert"]], axis=1), 0)
    b_router = _pad_lanes(jnp.concatenate([p["b_router_group"], p["b_router_expert"]])[None, :], 0)
    return {
        "g_mix": p["g_mix_norm"][None, :],
        "w_cat": w_cat.astype(BF16),
        "g_q_lora": p["g_q_lora"][None, :],
        "wq": wq.astype(BF16), "wqs": wqs.astype(BF16),
        "g_qk_q": _pad_lanes(p["g_qk_q"][None, :], 0),
        "g_kv": p["g_kv_lora"][None, :],
        "wk": wk.astype(BF16), "wv": wv.astype(BF16),
        "g_qk_k": _pad_lanes(p["g_qk_k"][None, :], 0),
        "w_pool": w_pool.astype(BF16),
        "pool_scale": p["pool_scale"][None, :],
        "w_gate": w_in[:, 1440:].astype(BF16),
        "w_br_pool": p["w_br_pool"].astype(BF16),
        "w_br_mla": p["w_br_mla"].astype(BF16),
        "w_br_sb": p["w_br_sb"].astype(BF16),
        "w_out": p["w_out"].astype(BF16),
        "g_ffn": p["g_ffn_norm"][None, :],
        "w_router": w_router.astype(BF16),
        "b_router": b_router,
        "w_exp_gate": p["w_exp_gate"].astype(BF16),
        "w_exp_up": p["w_exp_up"].astype(BF16),
        "w_exp_down": p["w_exp_down"].astype(BF16),
    }


def _rope_table(pos):
    half = MLA_ROPE // 2
    inv = ROPE_THETA ** (-jnp.arange(half, dtype=F32) / half)
    ang = pos.astype(F32)[:, None] * inv[None, :]
    cos, sin = jnp.cos(ang), jnp.sin(ang)
    n = pos.shape[0]
    cos_t = jnp.concatenate([jnp.ones((n, MLA_NOPE), F32), cos, cos, jnp.zeros((n, LANES - MLA_QK), F32)], axis=1)
    sin_t = jnp.concatenate([jnp.zeros((n, MLA_NOPE), F32), sin, sin, jnp.zeros((n, LANES - MLA_QK), F32)], axis=1)
    return jnp.stack([cos_t, sin_t])


def _pad_rows(a, rows):
    return a if a.shape[1] == rows else jnp.pad(a, ((0, 0), (0, rows - a.shape[1]), (0, 0)))


def _layer(x, cs, lw, hist, past):
    b, seq_len, _ = x.shape
    t = b * seq_len
    x2 = x.reshape(t, D_MODEL)
    up, q, lat, kr, qsb, ksb, vsb, ksbh, vsbh = _inproj(x2, cs, lw, seq_len)
    if hist is None:
        lk = seq_len
        kv_len = seq_len
        lat_all, kr_all, ksb_all, vsb_all = lat, kr, ksbh, vsbh
        hist16 = jnp.zeros((b, POOL_HALO, POOL_WIDTH), F32)
    else:
        kv_len = past + seq_len
        lk = -(-kv_len // 256) * 256
        cat = lambda old, new: _pad_rows(jnp.concatenate([old, new.reshape(b, seq_len, -1)], axis=1), lk)
        lat_all = cat(hist["lat"], lat).reshape(b * lk, MLA_KV_LORA)
        kr_old = jnp.pad(hist["kr"], ((0, 0), (0, 0), (MLA_NOPE, LANES - MLA_QK)))
        kr_all = cat(kr_old, kr).reshape(b * lk, LANES)
        ksb_all = cat(hist["sbk"].reshape(b, past, SB_WIDTH).astype(BF16), ksbh).reshape(b * lk, SB_WIDTH)
        vsb_all = cat(hist["sbv"].reshape(b, past, SB_WIDTH).astype(BF16), vsbh).reshape(b * lk, SB_WIDTH)
        hist16 = jnp.pad(hist["pool"], ((0, 0), (POOL_HALO - POOL_HIST, 0), (0, 0)))
    k_mla, v_mla = _mla_keys(lat_all, kr_all, lw)
    y_mla = _mla_attn(q, k_mla, v_mla, b, seq_len, lk, past, kv_len)
    y_sb = _sb_attn(qsb, ksb_all, vsb_all, b, seq_len, lk, past)
    y_pool = _pool(up, hist16, lw, b, seq_len, past)
    x1, xn2, comb = _merge(x2, y_pool, y_mla, y_sb, lw)
    x_out = _moe(xn2, comb, x1, lw).reshape(b, seq_len, D_MODEL)
    up3 = up.reshape(b, seq_len, POOL_WIDTH)
    if seq_len >= POOL_HIST:
        pool_state = up3[:, seq_len - POOL_HIST:]
    else:
        pool_state = jnp.concatenate([hist16[:, 1:], up3], axis=1)[:, -POOL_HIST:]
    state = (lat.reshape(b, seq_len, MLA_KV_LORA),
             kr[:, MLA_NOPE:MLA_QK].reshape(b, seq_len, MLA_ROPE),
             ksb.reshape(b, seq_len, SB_HEADS, SB_HEAD_DIM),
             vsb.reshape(b, seq_len, SB_HEADS, SB_HEAD_DIM),
             pool_state)
    return x_out, state


def kernel(x_prompt, x_sample, cache_mla_latent, cache_mla_krope, cache_sb_k, cache_sb_v, state_pool,
           g_mix_norm, w_in, w_pool_lin, pool_scale, g_q_lora, w_uq, g_kv_lora, w_ukv, g_qk_q, g_qk_k,
           w_br_pool, w_br_mla, w_br_sb, w_out, g_ffn_norm, w_router_group, b_router_group,
           w_router_expert, b_router_expert, w_exp_gate, w_exp_up, w_exp_down):
    depth = w_in.shape[0]
    past = cache_mla_latent.shape[2]
    cs_p = _rope_table(jnp.arange(x_prompt.shape[1], dtype=jnp.int32))
    cs_s = _rope_table(past + jnp.arange(x_sample.shape[1], dtype=jnp.int32))
    xp, xs = x_prompt, x_sample
    new_p = [[] for _ in range(5)]
    new_s = [[] for _ in range(5)]
    for l in range(depth):
        lw = _prep_layer({
            "g_mix_norm": g_mix_norm[l], "w_in": w_in[l], "w_pool_lin": w_pool_lin[l],
            "pool_scale": pool_scale[l], "g_q_lora": g_q_lora[l], "w_uq": w_uq[l],
            "g_kv_lora": g_kv_lora[l], "w_ukv": w_ukv[l], "g_qk_q": g_qk_q[l], "g_qk_k": g_qk_k[l],
            "w_br_pool": w_br_pool[l], "w_br_mla": w_br_mla[l], "w_br_sb": w_br_sb[l],
            "w_out": w_out[l], "g_ffn_norm": g_ffn_norm[l],
            "w_router_group": w_router_group[l], "b_router_group": b_router_group[l],
            "w_router_expert": w_router_expert[l], "b_router_expert": b_router_expert[l],
            "w_exp_gate": w_exp_gate[l], "w_exp_up": w_exp_up[l], "w_exp_down": w_exp_down[l],
        })
        hist = {"lat": cache_mla_latent[l], "kr": cache_mla_krope[l], "sbk": cache_sb_k[l],
                "sbv": cache_sb_v[l], "pool": state_pool[l]}
        xp, st_p = _layer(xp, cs_p, lw, None, 0)
        xs, st_s = _layer(xs, cs_s, lw, hist, past)
        for i in range(5):
            new_p[i].append(st_p[i])
            new_s[i].append(st_s[i])
    return (xp, xs, *[jnp.stack(s) for s in new_p], *[jnp.stack(s) for s in new_s])
```

```python
import functools

import jax
import jax.numpy as jnp
from jax import lax
from jax.experimental import pallas as pl
from jax.experimental.pallas import tpu as pltpu

F32 = jnp.float32
BF16 = jnp.bfloat16

D_MODEL = 1024
CHUNK_SHIFT = 6
EPS = 1e-6
POOL_WIDTH = 256
POOL_HIST = 15
POOL_HALO = 16
MLA_HEADS = 8
MLA_Q_LORA = 256
MLA_KV_LORA = 128
MLA_NOPE = 64
MLA_ROPE = 32
MLA_V = 64
MLA_QK = MLA_NOPE + MLA_ROPE
ROPE_THETA = 10000.0
SB_HEADS = 4
SB_HEAD_DIM = 64
SB_WIDTH = 256
N_GROUPS = 4
EXPERTS_PER_GROUP = 4
N_EXPERTS = 16
D_EXPERT = 256
LANES = 128
ROUTER_EXPERT_LANE0 = N_GROUPS
VMEM_LIMIT = 56 * 1024 * 1024
NEG_BIG = -1e30
LOG2_E = 1.4426950408889634

C_UPOOL = 0
C_CQ = 256
C_CKV = 512
C_QSB = 640
C_KSB = 896
C_VSB = 1152
C_KRA = 1408
C_KRB = 1536
C_TOTAL = 1664


def _tile(n, pref, mult=16):
    best = None
    for t in range(mult, min(n, pref) + 1, mult):
        if n % t == 0:
            best = t
    return best if best is not None else n


def _params(*sem):
    return pltpu.CompilerParams(dimension_semantics=sem, vmem_limit_bytes=VMEM_LIMIT)


def _rms(x, inv_n=None):
    if inv_n is None:
        ms = jnp.mean(x * x, axis=-1, keepdims=True)
    else:
        ms = jnp.sum(x * x, axis=-1, keepdims=True) * inv_n
    return x * lax.rsqrt(ms + EPS)


def _dot(a, b):
    return jnp.dot(a, b, preferred_element_type=F32)


def _dot_nt(a, b):
    return lax.dot_general(a, b, (((1,), (1,)), ((), ())), preferred_element_type=F32)


def _inproj_kernel(x_ref, g_ref, w_ref, cs_ref, gq_ref, wq_ref, wqs_ref, gqk_ref, gkv_ref,
                   up_ref, q_ref, lat_ref, kr_ref, qsb_ref, ksb_ref, vsb_ref, ksbh_ref, vsbh_ref):
    x = x_ref[...]
    xn = (_rms(x) * g_ref[...]).astype(BF16)
    z = _dot(xn, w_ref[...])
    up_ref[...] = z[:, C_UPOOL:C_UPOOL + POOL_WIDTH]
    cq = z[:, C_CQ:C_CQ + MLA_Q_LORA]
    cqn = (_rms(cq) * gq_ref[...]).astype(BF16)
    qa = _dot(cqn, wq_ref[...])
    qb = _dot(cqn, wqs_ref[...])
    cos = cs_ref[0]
    sin = cs_ref[1]
    gqk = gqk_ref[...]
    for h in range(MLA_HEADS):
        sl = slice(h * LANES, (h + 1) * LANES)
        qh = qa[:, sl] * cos + qb[:, sl] * sin
        q_ref[h] = (_rms(qh, 1.0 / MLA_QK) * gqk).astype(BF16)
    ckv = z[:, C_CKV:C_CKV + MLA_KV_LORA]
    lat_ref[...] = _rms(ckv) * gkv_ref[...]
    kr_ref[...] = z[:, C_KRA:C_KRA + LANES] * cos + z[:, C_KRB:C_KRB + LANES] * sin
    qsb_ref[...] = (z[:, C_QSB:C_QSB + SB_WIDTH] * (SB_HEAD_DIM ** -0.5)).astype(BF16)
    ksb = z[:, C_KSB:C_KSB + SB_WIDTH]
    ksb_ref[...] = ksb
    ksbh_ref[...] = ksb.astype(BF16)
    vsb = z[:, C_VSB:C_VSB + SB_WIDTH]
    vsb_ref[...] = vsb
    vsbh_ref[...] = vsb.astype(BF16)


def _inproj(x2, cs, lw, seq_len):
    t = x2.shape[0]
    tm = _tile(t, 512)
    if tm <= seq_len and seq_len % tm == 0:
        per = seq_len // tm
        cs_map = lambda i: (0, i % per, 0)
    else:
        assert tm % seq_len == 0
        cs = jnp.tile(cs, (1, tm // seq_len, 1))
        cs_map = lambda i: (0, 0, 0)
    row = lambda i: (i, 0)
    const2 = lambda i: (0, 0)
    out_shape = (
        jax.ShapeDtypeStruct((t, POOL_WIDTH), F32),
        jax.ShapeDtypeStruct((MLA_HEADS, t, LANES), BF16),
        jax.ShapeDtypeStruct((t, MLA_KV_LORA), F32),
        jax.ShapeDtypeStruct((t, LANES), F32),
        jax.ShapeDtypeStruct((t, SB_WIDTH), BF16),
        jax.ShapeDtypeStruct((t, SB_WIDTH), F32),
        jax.ShapeDtypeStruct((t, SB_WIDTH), F32),
        jax.ShapeDtypeStruct((t, SB_WIDTH), BF16),
        jax.ShapeDtypeStruct((t, SB_WIDTH), BF16),
    )
    out_specs = (
        pl.BlockSpec((tm, POOL_WIDTH), row),
        pl.BlockSpec((MLA_HEADS, tm, LANES), lambda i: (0, i, 0)),
        pl.BlockSpec((tm, MLA_KV_LORA), row),
        pl.BlockSpec((tm, LANES), row),
        pl.BlockSpec((tm, SB_WIDTH), row),
        pl.BlockSpec((tm, SB_WIDTH), row),
        pl.BlockSpec((tm, SB_WIDTH), row),
        pl.BlockSpec((tm, SB_WIDTH), row),
        pl.BlockSpec((tm, SB_WIDTH), row),
    )
    in_specs = [
        pl.BlockSpec((tm, D_MODEL), row),
        pl.BlockSpec((1, D_MODEL), const2),
        pl.BlockSpec((D_MODEL, C_TOTAL), const2),
        pl.BlockSpec((2, tm, LANES), cs_map),
        pl.BlockSpec((1, MLA_Q_LORA), const2),
        pl.BlockSpec((MLA_Q_LORA, MLA_HEADS * LANES), const2),
        pl.BlockSpec((MLA_Q_LORA, MLA_HEADS * LANES), const2),
        pl.BlockSpec((1, LANES), const2),
        pl.BlockSpec((1, MLA_KV_LORA), const2),
    ]
    return pl.pallas_call(
        _inproj_kernel, grid=(t // tm,), in_specs=in_specs, out_specs=out_specs, out_shape=out_shape,
        compiler_params=_params("parallel"), name="inproj",
    )(x2, lw["g_mix"], lw["w_cat"], cs, lw["g_q_lora"], lw["wq"], lw["wqs"], lw["g_qk_q"], lw["g_kv"])


def _mla_keys_kernel(lat_ref, kr_ref, wk_ref, wv_ref, g_ref, k_ref, v_ref):
    latb = lat_ref[...].astype(BF16)
    kn = _dot(latb, wk_ref[...])
    vv = _dot(latb, wv_ref[...])
    kr = kr_ref[...]
    g = g_ref[...]
    lane = lax.broadcasted_iota(jnp.int32, (1, LANES), 1)
    for h in range(MLA_HEADS):
        sl = slice(h * LANES, (h + 1) * LANES)
        kh = kn[:, sl] + kr
        k_ref[h] = (_rms(kh, 1.0 / MLA_QK) * g).astype(BF16)
        ones = (lane == _mla_ones_lane(h)).astype(F32)
        v_ref[h] = (vv[:, sl] + ones).astype(BF16)


def _mla_keys(lat2, kr2, lw):
    t = lat2.shape[0]
    tm = _tile(t, 1024)
    row = lambda i: (i, 0)
    const2 = lambda i: (0, 0)
    hd = lambda i: (0, i, 0)
    shp = jax.ShapeDtypeStruct((MLA_HEADS, t, LANES), BF16)
    return pl.pallas_call(
        _mla_keys_kernel, grid=(t // tm,),
        in_specs=[pl.BlockSpec((tm, MLA_KV_LORA), row), pl.BlockSpec((tm, LANES), row),
                  pl.BlockSpec((MLA_KV_LORA, MLA_HEADS * LANES), const2),
                  pl.BlockSpec((MLA_KV_LORA, MLA_HEADS * LANES), const2),
                  pl.BlockSpec((1, LANES), const2)],
        out_specs=(pl.BlockSpec((MLA_HEADS, tm, LANES), hd), pl.BlockSpec((MLA_HEADS, tm, LANES), hd)),
        out_shape=(shp, shp), compiler_params=_params("parallel"), name="mla_keys",
    )(lat2, kr2, lw["wk"], lw["wv"], lw["g_qk_k"])


def _mla_ones_lane(h):
    return MLA_V if h % 2 == 0 else 0


def _mla_attn_kernel(q_ref, k_ref, v_ref, o_ref, m_ref, acc_ref, *, tq, tk, q_off, kv_len):
    i = pl.program_id(2)
    q0 = q_off + i * tq
    vis_all = jnp.minimum(((q0 >> CHUNK_SHIFT) + 1) << CHUNK_SHIFT, kv_len)
    vis_any = jnp.minimum((((q0 + tq - 1) >> CHUNK_SHIFT) + 1) << CHUNK_SHIFT, kv_len)
    n_full = vis_all // tk
    n_blk = (vis_any + tk - 1) // tk
    c = (MLA_QK ** -0.5) * LOG2_E
    reps = tk // LANES
    m_ref[...] = jnp.full(m_ref.shape, NEG_BIG, F32)
    acc_ref[...] = jnp.zeros(acc_ref.shape, F32)

    def step(j, masked):
        start = pl.multiple_of(j * tk, tk)
        if masked:
            qpos = q0 + lax.broadcasted_iota(jnp.int32, (tq, tk), 0)
            kpos = start + lax.broadcasted_iota(jnp.int32, (tq, tk), 1)
            ok = ((kpos >> CHUNK_SHIFT) <= (qpos >> CHUNK_SHIFT)) & (kpos < kv_len)
        for hh in range(2):
            s = _dot_nt(q_ref[hh], k_ref[hh, pl.ds(start, tk), :])
            if masked:
                s = jnp.where(ok, s, -jnp.inf)
            m_prev = m_ref[hh]
            m_new = jnp.maximum(m_prev, jnp.max(s, axis=1, keepdims=True))
            alpha = jnp.exp2((m_prev - m_new) * c)
            p = jnp.exp2((s - jnp.tile(m_new, (1, reps))) * c)
            pv = _dot(p.astype(BF16), v_ref[hh, pl.ds(start, tk), :])
            acc_ref[hh] = alpha * acc_ref[hh] + pv
            m_ref[hh] = m_new

    def full_body(j, carry):
        step(j, False)
        return carry

    def masked_body(j, carry):
        step(j, True)
        return carry

    lax.fori_loop(0, n_full, full_body, 0)
    lax.fori_loop(n_full, n_blk, masked_body, 0)
    lane = lax.broadcasted_iota(jnp.int32, (1, LANES), 1)
    outs = []
    for hh in range(2):
        acc = acc_ref[hh]
        denom = jnp.sum(jnp.where(lane == _mla_ones_lane(hh), acc, 0.0), axis=1, keepdims=True)
        outs.append(acc / denom)
    o_ref[...] = jnp.where(lane < MLA_V, outs[0], outs[1]).astype(BF16)


def _mla_attn(q, k, v, batch, lq, lk, q_off, kv_len):
    tq = _tile(lq, 512)
    tk = _tile(lk, 512, LANES)
    assert tk % LANES == 0
    nq = lq // tq
    kern = functools.partial(_mla_attn_kernel, tq=tq, tk=tk, q_off=q_off, kv_len=kv_len)
    return pl.pallas_call(
        kern, grid=(batch, MLA_HEADS // 2, nq),
        in_specs=[pl.BlockSpec((2, tq, LANES), lambda b, p, i: (p, b * nq + i, 0)),
                  pl.BlockSpec((2, lk, LANES), lambda b, p, i: (p, b, 0)),
                  pl.BlockSpec((2, lk, LANES), lambda b, p, i: (p, b, 0))],
        out_specs=pl.BlockSpec((tq, LANES), lambda b, p, i: (b * nq + i, p)),
        out_shape=jax.ShapeDtypeStruct((batch * lq, MLA_HEADS * MLA_V), BF16),
        scratch_shapes=[pltpu.VMEM((2, tq, LANES), F32), pltpu.VMEM((2, tq, LANES), F32)],
        compiler_params=_params("parallel", "parallel", "arbitrary"), name="mla_attn",
    )(q, k, v)


def _sb_attn_kernel(q_ref, k_ref, v_ref, o_ref, c_ref, acc_ref, *, tq, tk, q_off):
    i = pl.program_id(2)
    q0 = q_off + i * tq
    n_full = q0 // tk
    n_blk = (q0 + tq - 2) // tk + 1
    lane = lax.broadcasted_iota(jnp.int32, (1, LANES), 1)
    upper = (lax.broadcasted_iota(jnp.int32, (tk, tk), 0) >
             lax.broadcasted_iota(jnp.int32, (tk, tk), 1)).astype(BF16)
    acc_ref[...] = jnp.zeros(acc_ref.shape, F32)
    c_ref[...] = jnp.zeros(c_ref.shape, F32)
    zero = jnp.zeros((), BF16)
    in_head = [(lane >= hh * SB_HEAD_DIM) & (lane < (hh + 1) * SB_HEAD_DIM) for hh in range(2)]
    qs = [jnp.where(in_head[hh], q_ref[...], zero) for hh in range(2)]

    def step(j, masked):
        start = pl.multiple_of(j * tk, tk)
        k = k_ref[pl.ds(start, tk), :]
        v = v_ref[pl.ds(start, tk), :]
        if masked:
            qpos = q0 + lax.broadcasted_iota(jnp.int32, (tq, tk), 0)
            kpos = start + lax.broadcasted_iota(jnp.int32, (tq, tk), 1)
            ok = kpos < qpos
        for hh in range(2):
            z2 = _dot_nt(qs[hh], k) * LOG2_E
            t = jnp.log2(1.0 + jnp.exp2(-jnp.abs(z2)))
            log_beta = jnp.minimum(z2, 0.0) - t
            log_keep = log_beta - z2
            if masked:
                log_keep = jnp.where(ok, log_keep, 0.0)
            hi = log_keep.astype(BF16)
            lo = (log_keep - hi.astype(F32)).astype(BF16)
            inside = _dot(hi, upper) + _dot(lo, upper)
            a = jnp.exp2(log_beta + inside)
            if masked:
                a = jnp.where(ok, a, 0.0)
            before = c_ref[hh]
            acc_ref[...] += jnp.exp2(before) * _dot(a.astype(BF16), jnp.where(in_head[hh], v, zero))
            c_ref[hh] = before + jnp.sum(log_keep, axis=1, keepdims=True)

    def masked_body(it, carry):
        step(n_blk - 1 - it, True)
        return carry

    def full_body(it, carry):
        step(n_full - 1 - it, False)
        return carry

    lax.fori_loop(0, n_blk - n_full, masked_body, 0)
    lax.fori_loop(0, n_full, full_body, 0)
    o_ref[...] = acc_ref[...].astype(BF16)


def _sb_attn(q, k, v, batch, lq, lk, q_off):
    tq = _tile(lq, 512)
    tk = _tile(lk, 256, LANES)
    nq = lq // tq
    kern = functools.partial(_sb_attn_kernel, tq=tq, tk=tk, q_off=q_off)
    return pl.pallas_call(
        kern, grid=(batch, SB_HEADS // 2, nq),
        in_specs=[pl.BlockSpec((tq, LANES), lambda b, p, i: (b * nq + i, p)),
                  pl.BlockSpec((lk, LANES), lambda b, p, i: (b, p)),
                  pl.BlockSpec((lk, LANES), lambda b, p, i: (b, p))],
        out_specs=pl.BlockSpec((tq, LANES), lambda b, p, i: (b * nq + i, p)),
        out_shape=jax.ShapeDtypeStruct((batch * lq, SB_WIDTH), BF16),
        scratch_shapes=[pltpu.VMEM((2, tq, LANES), F32), pltpu.VMEM((tq, LANES), F32)],
        compiler_params=_params("parallel", "parallel", "arbitrary"), name="sb_attn",
    )(q, k, v)


def _pool_kernel(u_ref, halo_ref, hist_ref, w_ref, sc_ref, y_ref, xb_ref, *, tm, past):
    i = pl.program_id(1)
    xb_ref[POOL_HALO:POOL_HALO + tm, :] = u_ref[...]

    @pl.when(i == 0)
    def _():
        xb_ref[0:POOL_HALO, :] = hist_ref[0]

    @pl.when(i > 0)
    def _():
        xb_ref[0:POOL_HALO, :] = halo_ref[...]

    def back(d):
        return xb_ref[POOL_HALO - d:POOL_HALO - d + tm, :]

    x0 = back(0)
    a2 = x0 + back(1)
    a4 = a2 + back(2) + back(3)
    a8 = a4 + back(4) + back(5) + back(6) + back(7)
    a16 = a8
    for d in range(8, 16):
        a16 = a16 + back(d)
    grp = lax.broadcasted_iota(jnp.int32, (1, POOL_WIDTH), 1) >> 6
    win = jnp.where(grp == 0, a2, jnp.where(grp == 1, a4, jnp.where(grp == 2, a8, a16)))
    width = jnp.where(grp == 0, 2.0, jnp.where(grp == 1, 4.0, jnp.where(grp == 2, 8.0, 16.0)))
    pos = past + i * tm + lax.broadcasted_iota(jnp.int32, (tm, 1), 0)
    cnt = jnp.minimum((pos + 1).astype(F32), width)
    d = win / cnt - x0
    y_ref[...] = (_dot(d.astype(BF16), w_ref[...]) * sc_ref[...]).astype(BF16)


def _pool(u2, hist16, lw, batch, seq_len, past):
    tm = _tile(seq_len, 512)
    nl = seq_len // tm
    per = tm // POOL_HALO
    kern = functools.partial(_pool_kernel, tm=tm, past=past)
    return pl.pallas_call(
        kern, grid=(batch, nl),
        in_specs=[pl.BlockSpec((tm, POOL_WIDTH), lambda b, i: (b * nl + i, 0)),
                  pl.BlockSpec((POOL_HALO, POOL_WIDTH),
                               lambda b, i: (jnp.maximum((b * nl + i) * per - 1, 0), 0)),
                  pl.BlockSpec((1, POOL_HALO, POOL_WIDTH), lambda b, i: (b, 0, 0)),
                  pl.BlockSpec((POOL_WIDTH, POOL_WIDTH), lambda b, i: (0, 0)),
                  pl.BlockSpec((1, POOL_WIDTH), lambda b, i: (0, 0))],
        out_specs=pl.BlockSpec((tm, POOL_WIDTH), lambda b, i: (b * nl + i, 0)),
        out_shape=jax.ShapeDtypeStruct((batch * seq_len, POOL_WIDTH), BF16),
        scratch_shapes=[pltpu.VMEM((POOL_HALO + tm, POOL_WIDTH), F32)],
        compiler_params=_params("parallel", "arbitrary"), name="pool",
    )(u2, u2, hist16, lw["w_pool"], lw["pool_scale"])


def _sigmoid(x):
    return 1.0 / (1.0 + jnp.exp(-x))


def _merge_kernel(x_ref, yp_ref, ym_ref, ys_ref, g1_ref, wg_ref, wbp_ref, wbm_ref, wbs_ref, wo_ref,
                  g2_ref, wr_ref, br_ref, x1_ref, xn2_ref, comb_ref):
    x = x_ref[...]
    xn = (_rms(x) * g1_ref[...]).astype(BF16)
    h = _sigmoid(_dot(xn, wg_ref[:, 0:D_MODEL])) * _dot(yp_ref[...], wbp_ref[...])
    h = h + _sigmoid(_dot(xn, wg_ref[:, D_MODEL:2 * D_MODEL])) * _dot(ym_ref[...], wbm_ref[...])
    h = h + _sigmoid(_dot(xn, wg_ref[:, 2 * D_MODEL:3 * D_MODEL])) * _dot(ys_ref[...], wbs_ref[...])
    x1 = x + _dot(h.astype(BF16), wo_ref[...])
    x1_ref[...] = x1
    xn2 = (_rms(x1) * g2_ref[...]).astype(BF16)
    xn2_ref[...] = xn2
    lg = _dot(xn2, wr_ref[...]) + br_ref[...]
    lane = lax.broadcasted_iota(jnp.int32, lg.shape, 1)
    lanef = lane.astype(F32)
    neg = -jnp.inf
    big = 1e9
    is_g = lane < N_GROUPS
    glog = jnp.where(is_g, lg, neg)
    gmax = jnp.max(glog, axis=1, keepdims=True)
    grp = jnp.min(jnp.where(glog == gmax, lanef, big), axis=1, keepdims=True)
    p_grp = 1.0 / jnp.sum(jnp.where(is_g, jnp.exp(lg - gmax), 0.0), axis=1, keepdims=True)
    eid = lanef - float(ROUTER_EXPERT_LANE0)
    lo = grp * float(EXPERTS_PER_GROUP)
    in_g = (eid >= lo) & (eid < lo + float(EXPERTS_PER_GROUP))
    e1 = jnp.where(in_g, lg, neg)
    v1 = jnp.max(e1, axis=1, keepdims=True)
    i1 = jnp.min(jnp.where(in_g & (e1 == v1), eid, big), axis=1, keepdims=True)
    rest = in_g & (eid != i1)
    e2 = jnp.where(rest, lg, neg)
    v2 = jnp.max(e2, axis=1, keepdims=True)
    i2 = jnp.min(jnp.where(rest & (e2 == v2), eid, big), axis=1, keepdims=True)
    ex = jnp.exp(v2 - v1)
    w1 = (1.0 / (1.0 + ex)) * p_grp
    w2 = (ex / (1.0 + ex)) * p_grp
    comb_ref[...] = jnp.where(eid == i1, w1, 0.0) + jnp.where(eid == i2, w2, 0.0)


def _merge(x2, yp, ym, ys, lw):
    t = x2.shape[0]
    tm = _tile(t, 256)
    row = lambda i: (i, 0)
    const2 = lambda i: (0, 0)
    return pl.pallas_call(
        _merge_kernel, grid=(t // tm,),
        in_specs=[pl.BlockSpec((tm, D_MODEL), row),
                  pl.BlockSpec((tm, POOL_WIDTH), row),
                  pl.BlockSpec((tm, MLA_HEADS * MLA_V), row),
                  pl.BlockSpec((tm, SB_WIDTH), row),
                  pl.BlockSpec((1, D_MODEL), const2),
                  pl.BlockSpec((D_MODEL, 3 * D_MODEL), const2),
                  pl.BlockSpec((POOL_WIDTH, D_MODEL), const2),
                  pl.BlockSpec((MLA_HEADS * MLA_V, D_MODEL), const2),
                  pl.BlockSpec((SB_WIDTH, D_MODEL), const2),
                  pl.BlockSpec((D_MODEL, D_MODEL), const2),
                  pl.BlockSpec((1, D_MODEL), const2),
                  pl.BlockSpec((D_MODEL, LANES), const2),
                  pl.BlockSpec((1, LANES), const2)],
        out_specs=(pl.BlockSpec((tm, D_MODEL), row), pl.BlockSpec((tm, D_MODEL), row),
                   pl.BlockSpec((tm, LANES), row)),
        out_shape=(jax.ShapeDtypeStruct((t, D_MODEL), F32), jax.ShapeDtypeStruct((t, D_MODEL), BF16),
                   jax.ShapeDtypeStruct((t, LANES), F32)),
        compiler_params=_params("parallel"), name="merge",
    )(x2, yp, ym, ys, lw["g_mix"], lw["w_gate"], lw["w_br_pool"], lw["w_br_mla"], lw["w_br_sb"],
      lw["w_out"], lw["g_ffn"], lw["w_router"], lw["b_router"])


def _moe_kernel(xn_ref, comb_ref, x1_ref, wg_ref, wu_ref, wd_ref, o_ref, acc_ref):
    e = pl.program_id(1)

    @pl.when(e == 0)
    def _():
        acc_ref[...] = jnp.zeros(acc_ref.shape, F32)

    xn = xn_ref[...]
    a = _dot(xn, wg_ref[0])
    h = (a * _sigmoid(a)) * _dot(xn, wu_ref[0])
    y = _dot(h.astype(BF16), wd_ref[0])
    comb = comb_ref[...]
    lane = lax.broadcasted_iota(jnp.int32, comb.shape, 1)
    c = jnp.sum(jnp.where(lane == e + ROUTER_EXPERT_LANE0, comb, 0.0), axis=1, keepdims=True)
    acc_ref[...] += c * y

    @pl.when(e == N_EXPERTS - 1)
    def _():
        o_ref[...] = x1_ref[...] + acc_ref[...]


def _moe(xn2, comb, x1, lw):
    t = xn2.shape[0]
    tm = _tile(t, 1024)
    row = lambda i, e: (i, 0)
    return pl.pallas_call(
        _moe_kernel, grid=(t // tm, N_EXPERTS),
        in_specs=[pl.BlockSpec((tm, D_MODEL), row), pl.BlockSpec((tm, LANES), row),
                  pl.BlockSpec((tm, D_MODEL), row),
                  pl.BlockSpec((1, D_MODEL, D_EXPERT), lambda i, e: (e, 0, 0)),
                  pl.BlockSpec((1, D_MODEL, D_EXPERT), lambda i, e: (e, 0, 0)),
                  pl.BlockSpec((1, D_EXPERT, D_MODEL), lambda i, e: (e, 0, 0))],
        out_specs=pl.BlockSpec((tm, D_MODEL), row),
        out_shape=jax.ShapeDtypeStruct((t, D_MODEL), F32),
        scratch_shapes=[pltpu.VMEM((tm, D_MODEL), F32)],
        compiler_params=_params("parallel", "arbitrary"), name="moe",
    )(xn2, comb, x1, lw["w_exp_gate"], lw["w_exp_up"], lw["w_exp_down"])


def _head_blocks(w, n_heads, width, place):
    k = w.shape[0]
    blocks = [place(h, w[:, h * width:(h + 1) * width]) for h in range(n_heads)]
    return jnp.concatenate(blocks, axis=1).reshape(k, n_heads * LANES)


def _pad_lanes(w, before, total=LANES):
    return jnp.pad(w, ((0, 0), (before, total - before - w.shape[1])))


def _swap_halves(w):
    half = w.shape[1] // 2
    return jnp.concatenate([-w[:, half:], w[:, :half]], axis=1)


def _prep_layer(p):
    w_in = p["w_in"]
    w_kr = w_in[:, 640:672]
    w_cat = jnp.concatenate(
        [w_in[:, 0:640], w_in[:, 672:1440],
         _pad_lanes(w_kr, MLA_NOPE), _pad_lanes(_swap_halves(w_kr), MLA_NOPE)], axis=1)
    wq = _head_blocks(p["w_uq"], MLA_HEADS, MLA_QK, lambda h, b: _pad_lanes(b, 0))
    wqs = _head_blocks(p["w_uq"], MLA_HEADS, MLA_QK,
                       lambda h, b: _pad_lanes(_swap_halves(b[:, MLA_NOPE:]), MLA_NOPE))
    wk = _head_blocks(p["w_ukv"], MLA_HEADS, MLA_NOPE + MLA_V,
                      lambda h, b: _pad_lanes(b[:, :MLA_NOPE], 0))
    wv = _head_blocks(p["w_ukv"], MLA_HEADS, MLA_NOPE + MLA_V,
                      lambda h, b: _pad_lanes(b[:, MLA_NOPE:], (h % 2) * MLA_V))
    w_pool = jax.scipy.linalg.block_diag(*[p["w_pool_lin"][g] for g in range(4)])
    w_router = _pad_lanes(jnp.concatenate([p["w_router_group"], p["w_router_expert"]], axis=1), 0)
    b_router = _pad_lanes(jnp.concatenate([p["b_router_group"], p["b_router_expert"]])[None, :], 0)
    return {
        "g_mix": p["g_mix_norm"][None, :],
        "w_cat": w_cat.astype(BF16),
        "g_q_lora": p["g_q_lora"][None, :],
        "wq": wq.astype(BF16), "wqs": wqs.astype(BF16),
        "g_qk_q": _pad_lanes(p["g_qk_q"][None, :], 0),
        "g_kv": p["g_kv_lora"][None, :],
        "wk": wk.astype(BF16), "wv": wv.astype(BF16),
        "g_qk_k": _pad_lanes(p["g_qk_k"][None, :], 0),
        "w_pool": w_pool.astype(BF16),
        "pool_scale": p["pool_scale"][None, :],
        "w_gate": w_in[:, 1440:].astype(BF16),
        "w_br_pool": p["w_br_pool"].astype(BF16),
        "w_br_mla": p["w_br_mla"].astype(BF16),
        "w_br_sb": p["w_br_sb"].astype(BF16),
        "w_out": p["w_out"].astype(BF16),
        "g_ffn": p["g_ffn_norm"][None, :],
        "w_router": w_router.astype(BF16),
        "b_router": b_router,
        "w_exp_gate": p["w_exp_gate"].astype(BF16),
        "w_exp_up": p["w_exp_up"].astype(BF16),
        "w_exp_down": p["w_exp_down"].astype(BF16),
    }


def _rope_table(pos):
    half = MLA_ROPE // 2
    inv = ROPE_THETA ** (-jnp.arange(half, dtype=F32) / half)
    ang = pos.astype(F32)[:, None] * inv[None, :]
    cos, sin = jnp.cos(ang), jnp.sin(ang)
    n = pos.shape[0]
    cos_t = jnp.concatenate([jnp.ones((n, MLA_NOPE), F32), cos, cos, jnp.zeros((n, LANES - MLA_QK), F32)], axis=1)
    sin_t = jnp.concatenate([jnp.zeros((n, MLA_NOPE), F32), sin, sin, jnp.zeros((n, LANES - MLA_QK), F32)], axis=1)
    return jnp.stack([cos_t, sin_t])


def _pad_rows(a, rows):
    return a if a.shape[1] == rows else jnp.pad(a, ((0, 0), (0, rows - a.shape[1]), (0, 0)))


def _layer(x, cs, lw, hist, past):
    b, seq_len, _ = x.shape
    t = b * seq_len
    x2 = x.reshape(t, D_MODEL)
    up, q, lat, kr, qsb, ksb, vsb, ksbh, vsbh = _inproj(x2, cs, lw, seq_len)
    if hist is None:
        lk = seq_len
        kv_len = seq_len
        lat_all, kr_all, ksb_all, vsb_all = lat, kr, ksbh, vsbh
        hist16 = jnp.zeros((b, POOL_HALO, POOL_WIDTH), F32)
    else:
        kv_len = past + seq_len
        lk = -(-kv_len // 256) * 256
        cat = lambda old, new: _pad_rows(jnp.concatenate([old, new.reshape(b, seq_len, -1)], axis=1), lk)
        lat_all = cat(hist["lat"], lat).reshape(b * lk, MLA_KV_LORA)
        kr_old = jnp.pad(hist["kr"], ((0, 0), (0, 0), (MLA_NOPE, LANES - MLA_QK)))
        kr_all = cat(kr_old, kr).reshape(b * lk, LANES)
        ksb_all = cat(hist["sbk"].reshape(b, past, SB_WIDTH).astype(BF16), ksbh).reshape(b * lk, SB_WIDTH)
        vsb_all = cat(hist["sbv"].reshape(b, past, SB_WIDTH).astype(BF16), vsbh).reshape(b * lk, SB_WIDTH)
        hist16 = jnp.pad(hist["pool"], ((0, 0), (POOL_HALO - POOL_HIST, 0), (0, 0)))
    k_mla, v_mla = _mla_keys(lat_all, kr_all, lw)
    y_mla = _mla_attn(q, k_mla, v_mla, b, seq_len, lk, past, kv_len)
    y_sb = _sb_attn(qsb, ksb_all, vsb_all, b, seq_len, lk, past)
    y_pool = _pool(up, hist16, lw, b, seq_len, past)
    x1, xn2, comb = _merge(x2, y_pool, y_mla, y_sb, lw)
    x_out = _moe(xn2, comb, x1, lw).reshape(b, seq_len, D_MODEL)
    up3 = up.reshape(b, seq_len, POOL_WIDTH)
    if seq_len >= POOL_HIST:
        pool_state = up3[:, seq_len - POOL_HIST:]
    else:
        pool_state = jnp.concatenate([hist16[:, 1:], up3], axis=1)[:, -POOL_HIST:]
    state = (lat.reshape(b, seq_len, MLA_KV_LORA),
             kr[:, MLA_NOPE:MLA_QK].reshape(b, seq_len, MLA_ROPE),
             ksb.reshape(b, seq_len, SB_HEADS, SB_HEAD_DIM),
             vsb.reshape(b, seq_len, SB_HEADS, SB_HEAD_DIM),
             pool_state)
    return x_out, state


def kernel(x_prompt, x_sample, cache_mla_latent, cache_mla_krope, cache_sb_k, cache_sb_v, state_pool,
           g_mix_norm, w_in, w_pool_lin, pool_scale, g_q_lora, w_uq, g_kv_lora, w_ukv, g_qk_q, g_qk_k,
           w_br_pool, w_br_mla, w_br_sb, w_out, g_ffn_norm, w_router_group, b_router_group,
           w_router_expert, b_router_expert, w_exp_gate, w_exp_up, w_exp_down):
    depth = w_in.shape[0]
    past = cache_mla_latent.shape[2]
    cs_p = _rope_table(jnp.arange(x_prompt.shape[1], dtype=jnp.int32))
    cs_s = _rope_table(past + jnp.arange(x_sample.shape[1], dtype=jnp.int32))
    xp, xs = x_prompt, x_sample
    new_p = [[] for _ in range(5)]
    new_s = [[] for _ in range(5)]
    for l in range(depth):
        lw = _prep_layer({
            "g_mix_norm": g_mix_norm[l], "w_in": w_in[l], "w_pool_lin": w_pool_lin[l],
            "pool_scale": pool_scale[l], "g_q_lora": g_q_lora[l], "w_uq": w_uq[l],
            "g_kv_lora": g_kv_lora[l], "w_ukv": w_ukv[l], "g_qk_q": g_qk_q[l], "g_qk_k": g_qk_k[l],
            "w_br_pool": w_br_pool[l], "w_br_mla": w_br_mla[l], "w_br_sb": w_br_sb[l],
            "w_out": w_out[l], "g_ffn_norm": g_ffn_norm[l],
            "w_router_group": w_router_group[l], "b_router_group": b_router_group[l],
            "w_router_expert": w_router_expert[l], "b_router_expert": b_router_expert[l],
            "w_exp_gate": w_exp_gate[l], "w_exp_up": w_exp_up[l], "w_exp_down": w_exp_down[l],
        })
        hist = {"lat": cache_mla_latent[l], "kr": cache_mla_krope[l], "sbk": cache_sb_k[l],
                "sbv": cache_sb_v[l], "pool": state_pool[l]}
        xp, st_p = _layer(xp, cs_p, lw, None, 0)
        xs, st_s = _layer(xs, cs_s, lw, hist, past)
        for i in range(5):
            new_p[i].append(st_p[i])
            new_s[i].append(st_s[i])
    return (xp, xs, *[jnp.stack(s) for s in new_p], *[jnp.stack(s) for s in new_s])
```

```python
import functools

import jax
import jax.numpy as jnp
from jax import lax
from jax.experimental import pallas as pl
from jax.experimental.pallas import tpu as pltpu

F32 = jnp.float32
BF16 = jnp.bfloat16

D_MODEL = 1024
CHUNK_SHIFT = 6
EPS = 1e-6
POOL_WIDTH = 256
POOL_HIST = 15
POOL_HALO = 16
MLA_HEADS = 8
MLA_Q_LORA = 256
MLA_KV_LORA = 128
MLA_NOPE = 64
MLA_ROPE = 32
MLA_V = 64
MLA_QK = MLA_NOPE + MLA_ROPE
ROPE_THETA = 10000.0
SB_HEADS = 4
SB_HEAD_DIM = 64
SB_WIDTH = 256
N_GROUPS = 4
EXPERTS_PER_GROUP = 4
N_EXPERTS = 16
D_EXPERT = 256
LANES = 128
ROUTER_EXPERT_LANE0 = N_GROUPS
VMEM_LIMIT = 56 * 1024 * 1024
NEG_BIG = -1e30
LOG2_E = 1.4426950408889634
SB_SUB = 256
ATTN_TILE_ELEMS = 512 * 512

C_UPOOL = 0
C_CQ = 256
C_CKV = 512
C_QSB = 640
C_KSB = 896
C_VSB = 1152
C_KRA = 1408
C_KRB = 1536
C_TOTAL = 1664


def _tile(n, pref, mult=16):
    best = None
    for t in range(mult, min(n, pref) + 1, mult):
        if n % t == 0:
            best = t
    return best if best is not None else n


def _params(*sem, flags=None):
    return pltpu.CompilerParams(dimension_semantics=sem, vmem_limit_bytes=VMEM_LIMIT, flags=flags)


def _rms(x, inv_n=None):
    if inv_n is None:
        ms = jnp.mean(x * x, axis=-1, keepdims=True)
    else:
        ms = jnp.sum(x * x, axis=-1, keepdims=True) * inv_n
    return x * lax.rsqrt(ms + EPS)


def _dot(a, b):
    return jnp.dot(a, b, preferred_element_type=F32)


def _dot_nt(a, b):
    return lax.dot_general(a, b, (((1,), (1,)), ((), ())), preferred_element_type=F32)


def _inproj_kernel(x_ref, g_ref, w_ref, cs_ref, gq_ref, wq_ref, wqs_ref, gqk_ref, gkv_ref,
                   up_ref, q_ref, lat_ref, kr_ref, qsb_ref, ksb_ref, vsb_ref, ksbh_ref, vsbh_ref):
    x = x_ref[...]
    xn = (_rms(x) * g_ref[...]).astype(BF16)
    z = _dot(xn, w_ref[...])
    up_ref[...] = z[:, C_UPOOL:C_UPOOL + POOL_WIDTH]
    cq = z[:, C_CQ:C_CQ + MLA_Q_LORA]
    cqn = (_rms(cq) * gq_ref[...]).astype(BF16)
    qa = _dot(cqn, wq_ref[...])
    qb = _dot(cqn, wqs_ref[...])
    cos = cs_ref[0]
    sin = cs_ref[1]
    gqk = gqk_ref[...]
    for h in range(MLA_HEADS):
        sl = slice(h * LANES, (h + 1) * LANES)
        qh = qa[:, sl] * cos + qb[:, sl] * sin
        q_ref[h] = (_rms(qh, 1.0 / MLA_QK) * gqk).astype(BF16)
    ckv = z[:, C_CKV:C_CKV + MLA_KV_LORA]
    lat_ref[...] = _rms(ckv) * gkv_ref[...]
    kr_ref[...] = z[:, C_KRA:C_KRA + LANES] * cos + z[:, C_KRB:C_KRB + LANES] * sin
    qsb_ref[...] = (z[:, C_QSB:C_QSB + SB_WIDTH] * (SB_HEAD_DIM ** -0.5)).astype(BF16)
    ksb = z[:, C_KSB:C_KSB + SB_WIDTH]
    ksb_ref[...] = ksb
    ksbh_ref[...] = ksb.astype(BF16)
    vsb = z[:, C_VSB:C_VSB + SB_WIDTH]
    vsb_ref[...] = vsb
    vsbh_ref[...] = vsb.astype(BF16)


def _inproj(x2, cs, lw, seq_len):
    t = x2.shape[0]
    tm = _tile(t, 512)
    if tm <= seq_len and seq_len % tm == 0:
        per = seq_len // tm
        cs_map = lambda i: (0, i % per, 0)
    else:
        assert tm % seq_len == 0
        cs = jnp.tile(cs, (1, tm // seq_len, 1))
        cs_map = lambda i: (0, 0, 0)
    row = lambda i: (i, 0)
    const2 = lambda i: (0, 0)
    out_shape = (
        jax.ShapeDtypeStruct((t, POOL_WIDTH), F32),
        jax.ShapeDtypeStruct((MLA_HEADS, t, LANES), BF16),
        jax.ShapeDtypeStruct((t, MLA_KV_LORA), F32),
        jax.ShapeDtypeStruct((t, LANES), F32),
        jax.ShapeDtypeStruct((t, SB_WIDTH), BF16),
        jax.ShapeDtypeStruct((t, SB_WIDTH), F32),
        jax.ShapeDtypeStruct((t, SB_WIDTH), F32),
        jax.ShapeDtypeStruct((t, SB_WIDTH), BF16),
        jax.ShapeDtypeStruct((t, SB_WIDTH), BF16),
    )
    out_specs = (
        pl.BlockSpec((tm, POOL_WIDTH), row),
        pl.BlockSpec((MLA_HEADS, tm, LANES), lambda i: (0, i, 0)),
        pl.BlockSpec((tm, MLA_KV_LORA), row),
        pl.BlockSpec((tm, LANES), row),
        pl.BlockSpec((tm, SB_WIDTH), row),
        pl.BlockSpec((tm, SB_WIDTH), row),
        pl.BlockSpec((tm, SB_WIDTH), row),
        pl.BlockSpec((tm, SB_WIDTH), row),
        pl.BlockSpec((tm, SB_WIDTH), row),
    )
    in_specs = [
        pl.BlockSpec((tm, D_MODEL), row),
        pl.BlockSpec((1, D_MODEL), const2),
        pl.BlockSpec((D_MODEL, C_TOTAL), const2),
        pl.BlockSpec((2, tm, LANES), cs_map),
        pl.BlockSpec((1, MLA_Q_LORA), const2),
        pl.BlockSpec((MLA_Q_LORA, MLA_HEADS * LANES), const2),
        pl.BlockSpec((MLA_Q_LORA, MLA_HEADS * LANES), const2),
        pl.BlockSpec((1, LANES), const2),
        pl.BlockSpec((1, MLA_KV_LORA), const2),
    ]
    return pl.pallas_call(
        _inproj_kernel, grid=(t // tm,), in_specs=in_specs, out_specs=out_specs, out_shape=out_shape,
        compiler_params=_params("parallel"), name="inproj",
    )(x2, lw["g_mix"], lw["w_cat"], cs, lw["g_q_lora"], lw["wq"], lw["wqs"], lw["g_qk_q"], lw["g_kv"])


def _mla_keys_kernel(lat_ref, kr_ref, wk_ref, wv_ref, g_ref, k_ref, v_ref):
    latb = lat_ref[...].astype(BF16)
    kn = _dot(latb, wk_ref[...])
    vv = _dot(latb, wv_ref[...])
    kr = kr_ref[...]
    g = g_ref[...]
    lane = lax.broadcasted_iota(jnp.int32, (1, LANES), 1)
    for h in range(MLA_HEADS):
        sl = slice(h * LANES, (h + 1) * LANES)
        kh = kn[:, sl] + kr
        k_ref[h] = (_rms(kh, 1.0 / MLA_QK) * g).astype(BF16)
        ones = (lane == _mla_ones_lane(h)).astype(F32)
        v_ref[h] = (vv[:, sl] + ones).astype(BF16)


def _mla_keys(lat2, kr2, lw):
    t = lat2.shape[0]
    tm = _tile(t, 1024)
    row = lambda i: (i, 0)
    const2 = lambda i: (0, 0)
    hd = lambda i: (0, i, 0)
    shp = jax.ShapeDtypeStruct((MLA_HEADS, t, LANES), BF16)
    return pl.pallas_call(
        _mla_keys_kernel, grid=(t // tm,),
        in_specs=[pl.BlockSpec((tm, MLA_KV_LORA), row), pl.BlockSpec((tm, LANES), row),
                  pl.BlockSpec((MLA_KV_LORA, MLA_HEADS * LANES), const2),
                  pl.BlockSpec((MLA_KV_LORA, MLA_HEADS * LANES), const2),
                  pl.BlockSpec((1, LANES), const2)],
        out_specs=(pl.BlockSpec((MLA_HEADS, tm, LANES), hd), pl.BlockSpec((MLA_HEADS, tm, LANES), hd)),
        out_shape=(shp, shp), compiler_params=_params("parallel"), name="mla_keys",
    )(lat2, kr2, lw["wk"], lw["wv"], lw["g_qk_k"])


def _mla_ones_lane(h):
    return MLA_V if h % 2 == 0 else 0


def _mla_attn_kernel(q_ref, k_ref, v_ref, o_ref, m_ref, acc_ref, *, tq, tk, q_off, kv_len):
    i = pl.program_id(2)
    q0 = q_off + i * tq
    vis_all = jnp.minimum(((q0 >> CHUNK_SHIFT) + 1) << CHUNK_SHIFT, kv_len)
    vis_any = jnp.minimum((((q0 + tq - 1) >> CHUNK_SHIFT) + 1) << CHUNK_SHIFT, kv_len)
    n_full = vis_all // tk
    n_blk = (vis_any + tk - 1) // tk
    c = (MLA_QK ** -0.5) * LOG2_E
    reps = tk // LANES
    m_ref[...] = jnp.full(m_ref.shape, NEG_BIG, F32)
    acc_ref[...] = jnp.zeros(acc_ref.shape, F32)

    def step(j, masked):
        start = pl.multiple_of(j * tk, tk)
        if masked:
            qpos = q0 + lax.broadcasted_iota(jnp.int32, (tq, tk), 0)
            kpos = start + lax.broadcasted_iota(jnp.int32, (tq, tk), 1)
            ok = ((kpos >> CHUNK_SHIFT) <= (qpos >> CHUNK_SHIFT)) & (kpos < kv_len)
        for hh in range(2):
            s = _dot_nt(q_ref[hh], k_ref[hh, pl.ds(start, tk), :])
            if masked:
                s = jnp.where(ok, s, -jnp.inf)
            m_prev = m_ref[hh]
            m_new = jnp.maximum(m_prev, jnp.max(s, axis=1, keepdims=True))
            alpha = jnp.exp2((m_prev - m_new) * c)
            p = jnp.exp2((s - jnp.tile(m_new, (1, reps))) * c)
            pv = _dot(p.astype(BF16), v_ref[hh, pl.ds(start, tk), :])
            acc_ref[hh] = alpha * acc_ref[hh] + pv
            m_ref[hh] = m_new

    def full_body(j, carry):
        step(j, False)
        return carry

    def masked_body(j, carry):
        step(j, True)
        return carry

    lax.fori_loop(0, n_full, full_body, 0)
    lax.fori_loop(n_full, n_blk, masked_body, 0)
    lane = lax.broadcasted_iota(jnp.int32, (1, LANES), 1)
    outs = []
    for hh in range(2):
        acc = acc_ref[hh]
        denom = jnp.sum(jnp.where(lane == _mla_ones_lane(hh), acc, 0.0), axis=1, keepdims=True)
        outs.append(acc / denom)
    o_ref[...] = jnp.where(lane < MLA_V, outs[0], outs[1]).astype(BF16)


def _mla_attn(q, k, v, batch, lq, lk, q_off, kv_len):
    tq = _tile(lq, 512)
    tk = _tile(lk, ATTN_TILE_ELEMS // tq, LANES)
    assert tk % LANES == 0
    nq = lq // tq
    kern = functools.partial(_mla_attn_kernel, tq=tq, tk=tk, q_off=q_off, kv_len=kv_len)
    return pl.pallas_call(
        kern, grid=(batch, MLA_HEADS // 2, nq),
        in_specs=[pl.BlockSpec((2, tq, LANES), lambda b, p, i: (p, b * nq + i, 0)),
                  pl.BlockSpec((2, lk, LANES), lambda b, p, i: (p, b, 0)),
                  pl.BlockSpec((2, lk, LANES), lambda b, p, i: (p, b, 0))],
        out_specs=pl.BlockSpec((tq, LANES), lambda b, p, i: (b * nq + i, p)),
        out_shape=jax.ShapeDtypeStruct((batch * lq, MLA_HEADS * MLA_V), BF16),
        scratch_shapes=[pltpu.VMEM((2, tq, LANES), F32), pltpu.VMEM((2, tq, LANES), F32)],
        compiler_params=_params("parallel", "parallel", "arbitrary"), name="mla_attn",
    )(q, k, v)


def _sb_attn_kernel(q_ref, k_ref, v_ref, o_ref, c_ref, acc_ref, *, tq, tk, q_off):
    i = pl.program_id(2)
    q0 = q_off + i * tq
    n_full = q0 // tk
    n_blk = (q0 + tq - 2) // tk + 1
    lane = lax.broadcasted_iota(jnp.int32, (1, LANES), 1)
    upper = (lax.broadcasted_iota(jnp.int32, (SB_SUB, SB_SUB), 0) >
             lax.broadcasted_iota(jnp.int32, (SB_SUB, SB_SUB), 1)).astype(BF16)
    upper2 = jnp.concatenate([upper, upper], axis=0)
    acc_ref[...] = jnp.zeros(acc_ref.shape, F32)
    c_ref[...] = jnp.zeros(c_ref.shape, F32)
    zero = jnp.zeros((), BF16)
    in_head = [(lane >= hh * SB_HEAD_DIM) & (lane < (hh + 1) * SB_HEAD_DIM) for hh in range(2)]
    qs = [jnp.where(in_head[hh], q_ref[...], zero) for hh in range(2)]

    def step(j, masked):
        start = pl.multiple_of(j * tk, tk)
        k = k_ref[pl.ds(start, tk), :]
        v = v_ref[pl.ds(start, tk), :]
        if masked:
            qpos = q0 + lax.broadcasted_iota(jnp.int32, (tq, tk), 0)
            kpos = start + lax.broadcasted_iota(jnp.int32, (tq, tk), 1)
            ok = kpos < qpos
        for hh in range(2):
            z = _dot_nt(qs[hh], k)
            t = jnp.log(1.0 + jnp.exp(-jnp.abs(z)))
            log_beta = jnp.minimum(z, 0.0) - t
            log_keep = log_beta - z
            if masked:
                log_keep = jnp.where(ok, log_keep, 0.0)
            hi = log_keep.astype(BF16)
            lo = (log_keep - hi.astype(F32)).astype(BF16)
            parts = []
            later = None
            for sb in reversed(range(tk // SB_SUB)):
                sl = slice(sb * SB_SUB, (sb + 1) * SB_SUB)
                inside = _dot(jnp.concatenate([hi[:, sl], lo[:, sl]], axis=1), upper2)
                if later is not None:
                    inside = inside + later
                parts.append(jnp.exp(log_beta[:, sl] + inside))
                total = jnp.sum(log_keep[:, sl], axis=1, keepdims=True)
                later = total if later is None else later + total
            a = parts[0] if len(parts) == 1 else jnp.concatenate(parts[::-1], axis=1)
            if masked:
                a = jnp.where(ok, a, 0.0)
            before = c_ref[hh]
            acc_ref[...] += jnp.exp(before) * _dot(a.astype(BF16), jnp.where(in_head[hh], v, zero))
            c_ref[hh] = before + later

    def masked_body(it, carry):
        step(n_blk - 1 - it, True)
        return carry

    def full_body(it, carry):
        step(n_full - 1 - it, False)
        return carry

    lax.fori_loop(0, n_blk - n_full, masked_body, 0)
    lax.fori_loop(0, n_full, full_body, 0)
    o_ref[...] = acc_ref[...].astype(BF16)


def _sb_attn(q, k, v, batch, lq, lk, q_off):
    tq = _tile(lq, 512)
    tk = _tile(lk, ATTN_TILE_ELEMS // tq, SB_SUB)
    assert tk % SB_SUB == 0
    nq = lq // tq
    kern = functools.partial(_sb_attn_kernel, tq=tq, tk=tk, q_off=q_off)
    return pl.pallas_call(
        kern, grid=(batch, SB_HEADS // 2, nq),
        in_specs=[pl.BlockSpec((tq, LANES), lambda b, p, i: (b * nq + i, p)),
                  pl.BlockSpec((lk, LANES), lambda b, p, i: (b, p)),
                  pl.BlockSpec((lk, LANES), lambda b, p, i: (b, p))],
        out_specs=pl.BlockSpec((tq, LANES), lambda b, p, i: (b * nq + i, p)),
        out_shape=jax.ShapeDtypeStruct((batch * lq, SB_WIDTH), BF16),
        scratch_shapes=[pltpu.VMEM((2, tq, LANES), F32), pltpu.VMEM((tq, LANES), F32)],
        compiler_params=_params("parallel", "parallel", "arbitrary"), name="sb_attn",
    )(q, k, v)


def _pool_kernel(u_ref, halo_ref, hist_ref, w_ref, sc_ref, y_ref, xb_ref, *, tm, past):
    i = pl.program_id(1)
    xb_ref[POOL_HALO:POOL_HALO + tm, :] = u_ref[...]

    @pl.when(i == 0)
    def _():
        xb_ref[0:POOL_HALO, :] = hist_ref[0]

    @pl.when(i > 0)
    def _():
        xb_ref[0:POOL_HALO, :] = halo_ref[...]

    def back(d):
        return xb_ref[POOL_HALO - d:POOL_HALO - d + tm, :]

    x0 = back(0)
    a2 = x0 + back(1)
    a4 = a2 + back(2) + back(3)
    a8 = a4 + back(4) + back(5) + back(6) + back(7)
    a16 = a8
    for d in range(8, 16):
        a16 = a16 + back(d)
    grp = lax.broadcasted_iota(jnp.int32, (1, POOL_WIDTH), 1) >> 6
    win = jnp.where(grp == 0, a2, jnp.where(grp == 1, a4, jnp.where(grp == 2, a8, a16)))
    width = jnp.where(grp == 0, 2.0, jnp.where(grp == 1, 4.0, jnp.where(grp == 2, 8.0, 16.0)))
    pos = past + i * tm + lax.broadcasted_iota(jnp.int32, (tm, 1), 0)
    cnt = jnp.minimum((pos + 1).astype(F32), width)
    d = win / cnt - x0
    y_ref[...] = (_dot(d.astype(BF16), w_ref[...]) * sc_ref[...]).astype(BF16)


def _pool(u2, hist16, lw, batch, seq_len, past):
    tm = _tile(seq_len, 512)
    nl = seq_len // tm
    per = tm // POOL_HALO
    kern = functools.partial(_pool_kernel, tm=tm, past=past)
    return pl.pallas_call(
        kern, grid=(batch, nl),
        in_specs=[pl.BlockSpec((tm, POOL_WIDTH), lambda b, i: (b * nl + i, 0)),
                  pl.BlockSpec((POOL_HALO, POOL_WIDTH),
                               lambda b, i: (jnp.maximum((b * nl + i) * per - 1, 0), 0)),
                  pl.BlockSpec((1, POOL_HALO, POOL_WIDTH), lambda b, i: (b, 0, 0)),
                  pl.BlockSpec((POOL_WIDTH, POOL_WIDTH), lambda b, i: (0, 0)),
                  pl.BlockSpec((1, POOL_WIDTH), lambda b, i: (0, 0))],
        out_specs=pl.BlockSpec((tm, POOL_WIDTH), lambda b, i: (b * nl + i, 0)),
        out_shape=jax.ShapeDtypeStruct((batch * seq_len, POOL_WIDTH), BF16),
        scratch_shapes=[pltpu.VMEM((POOL_HALO + tm, POOL_WIDTH), F32)],
        compiler_params=_params("parallel", "arbitrary"), name="pool",
    )(u2, u2, hist16, lw["w_pool"], lw["pool_scale"])


def _sigmoid(x):
    return 1.0 / (1.0 + jnp.exp(-x))


def _merge_kernel(x_ref, yp_ref, ym_ref, ys_ref, g1_ref, wg_ref, wbp_ref, wbm_ref, wbs_ref, wo_ref,
                  g2_ref, wr_ref, br_ref, x1_ref, xn2_ref, comb_ref):
    x = x_ref[...]
    xn = (_rms(x) * g1_ref[...]).astype(BF16)
    h = _sigmoid(_dot(xn, wg_ref[:, 0:D_MODEL])) * _dot(yp_ref[...], wbp_ref[...])
    h = h + _sigmoid(_dot(xn, wg_ref[:, D_MODEL:2 * D_MODEL])) * _dot(ym_ref[...], wbm_ref[...])
    h = h + _sigmoid(_dot(xn, wg_ref[:, 2 * D_MODEL:3 * D_MODEL])) * _dot(ys_ref[...], wbs_ref[...])
    x1 = x + _dot(h.astype(BF16), wo_ref[...])
    x1_ref[...] = x1
    xn2 = (_rms(x1) * g2_ref[...]).astype(BF16)
    xn2_ref[...] = xn2
    lg = _dot(xn2, wr_ref[...]) + br_ref[...]
    lane = lax.broadcasted_iota(jnp.int32, lg.shape, 1)
    lanef = lane.astype(F32)
    neg = -jnp.inf
    big = 1e9
    is_g = lane < N_GROUPS
    glog = jnp.where(is_g, lg, neg)
    gmax = jnp.max(glog, axis=1, keepdims=True)
    grp = jnp.min(jnp.where(glog == gmax, lanef, big), axis=1, keepdims=True)
    p_grp = 1.0 / jnp.sum(jnp.where(is_g, jnp.exp(lg - gmax), 0.0), axis=1, keepdims=True)
    eid = lanef - float(ROUTER_EXPERT_LANE0)
    lo = grp * float(EXPERTS_PER_GROUP)
    in_g = (eid >= lo) & (eid < lo + float(EXPERTS_PER_GROUP))
    e1 = jnp.where(in_g, lg, neg)
    v1 = jnp.max(e1, axis=1, keepdims=True)
    i1 = jnp.min(jnp.where(in_g & (e1 == v1), eid, big), axis=1, keepdims=True)
    rest = in_g & (eid != i1)
    e2 = jnp.where(rest, lg, neg)
    v2 = jnp.max(e2, axis=1, keepdims=True)
    i2 = jnp.min(jnp.where(rest & (e2 == v2), eid, big), axis=1, keepdims=True)
    ex = jnp.exp(v2 - v1)
    w1 = (1.0 / (1.0 + ex)) * p_grp
    w2 = (ex / (1.0 + ex)) * p_grp
    comb_ref[...] = jnp.where(eid == i1, w1, 0.0) + jnp.where(eid == i2, w2, 0.0)


def _merge(x2, yp, ym, ys, lw):
    t = x2.shape[0]
    tm = _tile(t, 512)
    row = lambda i: (i, 0)
    const2 = lambda i: (0, 0)
    return pl.pallas_call(
        _merge_kernel, grid=(t // tm,),
        in_specs=[pl.BlockSpec((tm, D_MODEL), row),
                  pl.BlockSpec((tm, POOL_WIDTH), row),
                  pl.BlockSpec((tm, MLA_HEADS * MLA_V), row),
                  pl.BlockSpec((tm, SB_WIDTH), row),
                  pl.BlockSpec((1, D_MODEL), const2),
                  pl.BlockSpec((D_MODEL, 3 * D_MODEL), const2),
                  pl.BlockSpec((POOL_WIDTH, D_MODEL), const2),
                  pl.BlockSpec((MLA_HEADS * MLA_V, D_MODEL), const2),
                  pl.BlockSpec((SB_WIDTH, D_MODEL), const2),
                  pl.BlockSpec((D_MODEL, D_MODEL), const2),
                  pl.BlockSpec((1, D_MODEL), const2),
                  pl.BlockSpec((D_MODEL, LANES), const2),
                  pl.BlockSpec((1, LANES), const2)],
        out_specs=(pl.BlockSpec((tm, D_MODEL), row), pl.BlockSpec((tm, D_MODEL), row),
                   pl.BlockSpec((tm, LANES), row)),
        out_shape=(jax.ShapeDtypeStruct((t, D_MODEL), F32), jax.ShapeDtypeStruct((t, D_MODEL), BF16),
                   jax.ShapeDtypeStruct((t, LANES), F32)),
        compiler_params=_params("parallel"), name="merge",
    )(x2, yp, ym, ys, lw["g_mix"], lw["w_gate"], lw["w_br_pool"], lw["w_br_mla"], lw["w_br_sb"],
      lw["w_out"], lw["g_ffn"], lw["w_router"], lw["b_router"])


def _moe_kernel(xn_ref, comb_ref, x1_ref, wg_ref, wu_ref, wd_ref, o_ref):
    xn = xn_ref[...]
    comb = comb_ref[...]
    group_w = EXPERTS_PER_GROUP * D_EXPERT
    y = x1_ref[...]
    for g in range(N_GROUPS):
        cols = slice(g * group_w, (g + 1) * group_w)
        a = _dot(xn, wg_ref[:, cols])
        h = (a * _sigmoid(a)) * _dot(xn, wu_ref[:, cols])
        scaled = []
        for j in range(EXPERTS_PER_GROUP):
            lane = ROUTER_EXPERT_LANE0 + g * EXPERTS_PER_GROUP + j
            scaled.append((h[:, j * D_EXPERT:(j + 1) * D_EXPERT] * comb[:, lane:lane + 1]).astype(BF16))
        y = y + _dot(jnp.concatenate(scaled, axis=1), wd_ref[cols, :])
    o_ref[...] = y


def _moe(xn2, comb, x1, lw):
    t = xn2.shape[0]
    tm = _tile(t, 512)
    row = lambda i: (i, 0)
    const2 = lambda i: (0, 0)
    hidden = N_EXPERTS * D_EXPERT
    resident = pl.Buffered(1)
    return pl.pallas_call(
        _moe_kernel, grid=(t // tm,),
        in_specs=[pl.BlockSpec((tm, D_MODEL), row), pl.BlockSpec((tm, LANES), row),
                  pl.BlockSpec((tm, D_MODEL), row),
                  pl.BlockSpec((D_MODEL, hidden), const2, pipeline_mode=resident),
                  pl.BlockSpec((D_MODEL, hidden), const2, pipeline_mode=resident),
                  pl.BlockSpec((hidden, D_MODEL), const2, pipeline_mode=resident)],
        out_specs=pl.BlockSpec((tm, D_MODEL), row),
        out_shape=jax.ShapeDtypeStruct((t, D_MODEL), F32),
        compiler_params=_params("parallel"), name="moe",
    )(xn2, comb, x1, lw["w_exp_gate"], lw["w_exp_up"], lw["w_exp_down"])


def _head_blocks(w, n_heads, width, place):
    k = w.shape[0]
    blocks = [place(h, w[:, h * width:(h + 1) * width]) for h in range(n_heads)]
    return jnp.concatenate(blocks, axis=1).reshape(k, n_heads * LANES)


def _pad_lanes(w, before, total=LANES):
    return jnp.pad(w, ((0, 0), (before, total - before - w.shape[1])))


def _swap_halves(w):
    half = w.shape[1] // 2
    return jnp.concatenate([-w[:, half:], w[:, :half]], axis=1)


def _experts_side_by_side(w):
    e, d, f = w.shape
    return jnp.transpose(w.astype(BF16), (1, 0, 2)).reshape(d, e * f)


def _prep_layer(p):
    w_in = p["w_in"]
    w_kr = w_in[:, 640:672]
    w_cat = jnp.concatenate(
        [w_in[:, 0:640], w_in[:, 672:1440],
         _pad_lanes(w_kr, MLA_NOPE), _pad_lanes(_swap_halves(w_kr), MLA_NOPE)], axis=1)
    wq = _head_blocks(p["w_uq"], MLA_HEADS, MLA_QK, lambda h, b: _pad_lanes(b, 0))
    wqs = _head_blocks(p["w_uq"], MLA_HEADS, MLA_QK,
                       lambda h, b: _pad_lanes(_swap_halves(b[:, MLA_NOPE:]), MLA_NOPE))
    wk = _head_blocks(p["w_ukv"], MLA_HEADS, MLA_NOPE + MLA_V,
                      lambda h, b: _pad_lanes(b[:, :MLA_NOPE], 0))
    wv = _head_blocks(p["w_ukv"], MLA_HEADS, MLA_NOPE + MLA_V,
                      lambda h, b: _pad_lanes(b[:, MLA_NOPE:], (h % 2) * MLA_V))
    w_pool = jax.scipy.linalg.block_diag(*[p["w_pool_lin"][g] for g in range(4)])
    w_router = _pad_lanes(jnp.concatenate([p["w_router_group"], p["w_router_expert"]], axis=1), 0)
    b_router = _pad_lanes(jnp.concatenate([p["b_router_group"], p["b_router_expert"]])[None, :], 0)
    return {
        "g_mix": p["g_mix_norm"][None, :],
        "w_cat": w_cat.astype(BF16),
        "g_q_lora": p["g_q_lora"][None, :],
        "wq": wq.astype(BF16), "wqs": wqs.astype(BF16),
        "g_qk_q": _pad_lanes(p["g_qk_q"][None, :], 0),
        "g_kv": p["g_kv_lora"][None, :],
        "wk": wk.astype(BF16), "wv": wv.astype(BF16),
        "g_qk_k": _pad_lanes(p["g_qk_k"][None, :], 0),
        "w_pool": w_pool.astype(BF16),
        "pool_scale": p["pool_scale"][None, :],
        "w_gate": w_in[:, 1440:].astype(BF16),
        "w_br_pool": p["w_br_pool"].astype(BF16),
        "w_br_mla": p["w_br_mla"].astype(BF16),
        "w_br_sb": p["w_br_sb"].astype(BF16),
        "w_out": p["w_out"].astype(BF16),
        "g_ffn": p["g_ffn_norm"][None, :],
        "w_router": w_router.astype(BF16),
        "b_router": b_router,
        "w_exp_gate": _experts_side_by_side(p["w_exp_gate"]),
        "w_exp_up": _experts_side_by_side(p["w_exp_up"]),
        "w_exp_down": p["w_exp_down"].astype(BF16).reshape(N_EXPERTS * D_EXPERT, D_MODEL),
    }


def _rope_table(pos):
    half = MLA_ROPE // 2
    inv = ROPE_THETA ** (-jnp.arange(half, dtype=F32) / half)
    ang = pos.astype(F32)[:, None] * inv[None, :]
    cos, sin = jnp.cos(ang), jnp.sin(ang)
    n = pos.shape[0]
    cos_t = jnp.concatenate([jnp.ones((n, MLA_NOPE), F32), cos, cos, jnp.zeros((n, LANES - MLA_QK), F32)], axis=1)
    sin_t = jnp.concatenate([jnp.zeros((n, MLA_NOPE), F32), sin, sin, jnp.zeros((n, LANES - MLA_QK), F32)], axis=1)
    return jnp.stack([cos_t, sin_t])


def _pad_rows(a, rows):
    return a if a.shape[1] == rows else jnp.pad(a, ((0, 0), (0, rows - a.shape[1]), (0, 0)))


def _layer(x, cs, lw, hist, past):
    b, seq_len, _ = x.shape
    t = b * seq_len
    x2 = x.reshape(t, D_MODEL)
    up, q, lat, kr, qsb, ksb, vsb, ksbh, vsbh = _inproj(x2, cs, lw, seq_len)
    if hist is None:
        lk = seq_len
        kv_len = seq_len
        lat_all, kr_all, ksb_all, vsb_all = lat, kr, ksbh, vsbh
        hist16 = jnp.zeros((b, POOL_HALO, POOL_WIDTH), F32)
    else:
        kv_len = past + seq_len
        lk = -(-kv_len // 256) * 256
        cat = lambda old, new: _pad_rows(jnp.concatenate([old, new.reshape(b, seq_len, -1)], axis=1), lk)
        lat_all = cat(hist["lat"], lat).reshape(b * lk, MLA_KV_LORA)
        kr_old = jnp.pad(hist["kr"], ((0, 0), (0, 0), (MLA_NOPE, LANES - MLA_QK)))
        kr_all = cat(kr_old, kr).reshape(b * lk, LANES)
        ksb_all = cat(hist["sbk"].reshape(b, past, SB_WIDTH).astype(BF16), ksbh).reshape(b * lk, SB_WIDTH)
        vsb_all = cat(hist["sbv"].reshape(b, past, SB_WIDTH).astype(BF16), vsbh).reshape(b * lk, SB_WIDTH)
        hist16 = jnp.pad(hist["pool"], ((0, 0), (POOL_HALO - POOL_HIST, 0), (0, 0)))
    k_mla, v_mla = _mla_keys(lat_all, kr_all, lw)
    y_mla = _mla_attn(q, k_mla, v_mla, b, seq_len, lk, past, kv_len)
    y_sb = _sb_attn(qsb, ksb_all, vsb_all, b, seq_len, lk, past)
    y_pool = _pool(up, hist16, lw, b, seq_len, past)
    x1, xn2, comb = _merge(x2, y_pool, y_mla, y_sb, lw)
    x_out = _moe(xn2, comb, x1, lw).reshape(b, seq_len, D_MODEL)
    up3 = up.reshape(b, seq_len, POOL_WIDTH)
    if seq_len >= POOL_HIST:
        pool_state = up3[:, seq_len - POOL_HIST:]
    else:
        pool_state = jnp.concatenate([hist16[:, 1:], up3], axis=1)[:, -POOL_HIST:]
    state = (lat.reshape(b, seq_len, MLA_KV_LORA),
             kr[:, MLA_NOPE:MLA_QK].reshape(b, seq_len, MLA_ROPE),
             ksb.reshape(b, seq_len, SB_HEADS, SB_HEAD_DIM),
             vsb.reshape(b, seq_len, SB_HEADS, SB_HEAD_DIM),
             pool_state)
    return x_out, state


def kernel(x_prompt, x_sample, cache_mla_latent, cache_mla_krope, cache_sb_k, cache_sb_v, state_pool,
           g_mix_norm, w_in, w_pool_lin, pool_scale, g_q_lora, w_uq, g_kv_lora, w_ukv, g_qk_q, g_qk_k,
           w_br_pool, w_br_mla, w_br_sb, w_out, g_ffn_norm, w_router_group, b_router_group,
           w_router_expert, b_router_expert, w_exp_gate, w_exp_up, w_exp_down):
    depth = w_in.shape[0]
    past = cache_mla_latent.shape[2]
    cs_p = _rope_table(jnp.arange(x_prompt.shape[1], dtype=jnp.int32))
    cs_s = _rope_table(past + jnp.arange(x_sample.shape[1], dtype=jnp.int32))
    xp, xs = x_prompt, x_sample
    new_p = [[] for _ in range(5)]
    new_s = [[] for _ in range(5)]
    for l in range(depth):
        lw = _prep_layer({
            "g_mix_norm": g_mix_norm[l], "w_in": w_in[l], "w_pool_lin": w_pool_lin[l],
            "pool_scale": pool_scale[l], "g_q_lora": g_q_lora[l], "w_uq": w_uq[l],
            "g_kv_lora": g_kv_lora[l], "w_ukv": w_ukv[l], "g_qk_q": g_qk_q[l], "g_qk_k": g_qk_k[l],
            "w_br_pool": w_br_pool[l], "w_br_mla": w_br_mla[l], "w_br_sb": w_br_sb[l],
            "w_out": w_out[l], "g_ffn_norm": g_ffn_norm[l],
            "w_router_group": w_router_group[l], "b_router_group": b_router_group[l],
            "w_router_expert": w_router_expert[l], "b_router_expert": b_router_expert[l],
            "w_exp_gate": w_exp_gate[l], "w_exp_up": w_exp_up[l], "w_exp_down": w_exp_down[l],
        })
        hist = {"lat": cache_mla_latent[l], "kr": cache_mla_krope[l], "sbk": cache_sb_k[l],
                "sbv": cache_sb_v[l], "pool": state_pool[l]}
        xp, st_p = _layer(xp, cs_p, lw, None, 0)
        xs, st_s = _layer(xs, cs_s, lw, hist, past)
        for i in range(5):
            new_p[i].append(st_p[i])
            new_s[i].append(st_s[i])
    return (xp, xs, *[jnp.stack(s) for s in new_p], *[jnp.stack(s) for s in new_s])
```

```python
import functools

import jax
import jax.numpy as jnp
from jax import lax
from jax.experimental import pallas as pl
from jax.experimental.pallas import tpu as pltpu

F32 = jnp.float32
BF16 = jnp.bfloat16

D_MODEL = 1024
CHUNK_SHIFT = 6
EPS = 1e-6
POOL_WIDTH = 256
POOL_HIST = 15
POOL_HALO = 16
MLA_HEADS = 8
MLA_Q_LORA = 256
MLA_KV_LORA = 128
MLA_NOPE = 64
MLA_ROPE = 32
MLA_V = 64
MLA_QK = MLA_NOPE + MLA_ROPE
ROPE_THETA = 10000.0
SB_HEADS = 4
SB_HEAD_DIM = 64
SB_WIDTH = 256
N_GROUPS = 4
EXPERTS_PER_GROUP = 4
N_EXPERTS = 16
D_EXPERT = 256
LANES = 128
ROUTER_EXPERT_LANE0 = N_GROUPS
VMEM_LIMIT = 56 * 1024 * 1024
NEG_BIG = -1e30
LOG2_E = 1.4426950408889634
SB_SUB = 256
ATTN_TILE_ELEMS = 512 * 512
MLA_BOUND_SLACK = 1.02
MLA_SAFE_BOUND = 60.0

C_UPOOL = 0
C_CQ = 256
C_CKV = 512
C_QSB = 640
C_KSB = 896
C_VSB = 1152
C_KRA = 1408
C_KRB = 1536
C_TOTAL = 1664


def _tile(n, pref, mult=16):
    best = None
    for t in range(mult, min(n, pref) + 1, mult):
        if n % t == 0:
            best = t
    return best if best is not None else n


def _params(*sem, flags=None):
    return pltpu.CompilerParams(dimension_semantics=sem, vmem_limit_bytes=VMEM_LIMIT, flags=flags)


def _rms(x, inv_n=None):
    if inv_n is None:
        ms = jnp.mean(x * x, axis=-1, keepdims=True)
    else:
        ms = jnp.sum(x * x, axis=-1, keepdims=True) * inv_n
    return x * lax.rsqrt(ms + EPS)


def _dot(a, b):
    return jnp.dot(a, b, preferred_element_type=F32)


def _dot_nt(a, b):
    return lax.dot_general(a, b, (((1,), (1,)), ((), ())), preferred_element_type=F32)


def _inproj_kernel(x_ref, g_ref, w_ref, cs_ref, gq_ref, wq_ref, wqs_ref, gqk_ref, gkv_ref,
                   up_ref, q_ref, lat_ref, kr_ref, qsb_ref, ksb_ref, vsb_ref, ksbh_ref, vsbh_ref):
    x = x_ref[...]
    xn = (_rms(x) * g_ref[...]).astype(BF16)
    z = _dot(xn, w_ref[...])
    up_ref[...] = z[:, C_UPOOL:C_UPOOL + POOL_WIDTH]
    cq = z[:, C_CQ:C_CQ + MLA_Q_LORA]
    cqn = (_rms(cq) * gq_ref[...]).astype(BF16)
    qa = _dot(cqn, wq_ref[...])
    qb = _dot(cqn, wqs_ref[...])
    cos = cs_ref[0]
    sin = cs_ref[1]
    gqk = gqk_ref[...]
    for h in range(MLA_HEADS):
        sl = slice(h * LANES, (h + 1) * LANES)
        qh = qa[:, sl] * cos + qb[:, sl] * sin
        q_ref[h] = (_rms(qh, 1.0 / MLA_QK) * gqk).astype(BF16)
    ckv = z[:, C_CKV:C_CKV + MLA_KV_LORA]
    lat_ref[...] = _rms(ckv) * gkv_ref[...]
    kr_ref[...] = z[:, C_KRA:C_KRA + LANES] * cos + z[:, C_KRB:C_KRB + LANES] * sin
    qsb_ref[...] = (z[:, C_QSB:C_QSB + SB_WIDTH] * (SB_HEAD_DIM ** -0.5)).astype(BF16)
    ksb = z[:, C_KSB:C_KSB + SB_WIDTH]
    ksb_ref[...] = ksb
    ksbh_ref[...] = ksb.astype(BF16)
    vsb = z[:, C_VSB:C_VSB + SB_WIDTH]
    vsb_ref[...] = vsb
    vsbh_ref[...] = vsb.astype(BF16)


def _inproj(x2, cs, lw, seq_len):
    t = x2.shape[0]
    tm = _tile(t, 512)
    if tm <= seq_len and seq_len % tm == 0:
        per = seq_len // tm
        cs_map = lambda i: (0, i % per, 0)
    else:
        assert tm % seq_len == 0
        cs = jnp.tile(cs, (1, tm // seq_len, 1))
        cs_map = lambda i: (0, 0, 0)
    row = lambda i: (i, 0)
    const2 = lambda i: (0, 0)
    out_shape = (
        jax.ShapeDtypeStruct((t, POOL_WIDTH), F32),
        jax.ShapeDtypeStruct((MLA_HEADS, t, LANES), BF16),
        jax.ShapeDtypeStruct((t, MLA_KV_LORA), F32),
        jax.ShapeDtypeStruct((t, LANES), F32),
        jax.ShapeDtypeStruct((t, SB_WIDTH), BF16),
        jax.ShapeDtypeStruct((t, SB_WIDTH), F32),
        jax.ShapeDtypeStruct((t, SB_WIDTH), F32),
        jax.ShapeDtypeStruct((t, SB_WIDTH), BF16),
        jax.ShapeDtypeStruct((t, SB_WIDTH), BF16),
    )
    out_specs = (
        pl.BlockSpec((tm, POOL_WIDTH), row),
        pl.BlockSpec((MLA_HEADS, tm, LANES), lambda i: (0, i, 0)),
        pl.BlockSpec((tm, MLA_KV_LORA), row),
        pl.BlockSpec((tm, LANES), row),
        pl.BlockSpec((tm, SB_WIDTH), row),
        pl.BlockSpec((tm, SB_WIDTH), row),
        pl.BlockSpec((tm, SB_WIDTH), row),
        pl.BlockSpec((tm, SB_WIDTH), row),
        pl.BlockSpec((tm, SB_WIDTH), row),
    )
    in_specs = [
        pl.BlockSpec((tm, D_MODEL), row),
        pl.BlockSpec((1, D_MODEL), const2),
        pl.BlockSpec((D_MODEL, C_TOTAL), const2),
        pl.BlockSpec((2, tm, LANES), cs_map),
        pl.BlockSpec((1, MLA_Q_LORA), const2),
        pl.BlockSpec((MLA_Q_LORA, MLA_HEADS * LANES), const2),
        pl.BlockSpec((MLA_Q_LORA, MLA_HEADS * LANES), const2),
        pl.BlockSpec((1, LANES), const2),
        pl.BlockSpec((1, MLA_KV_LORA), const2),
    ]
    return pl.pallas_call(
        _inproj_kernel, grid=(t // tm,), in_specs=in_specs, out_specs=out_specs, out_shape=out_shape,
        compiler_params=_params("parallel"), name="inproj",
    )(x2, lw["g_mix"], lw["w_cat"], cs, lw["g_q_lora"], lw["wq"], lw["wqs"], lw["g_qk_q"], lw["g_kv"])


def _mla_keys_kernel(lat_ref, kr_ref, wk_ref, wv_ref, g_ref, k_ref, v_ref):
    latb = lat_ref[...].astype(BF16)
    kn = _dot(latb, wk_ref[...])
    vv = _dot(latb, wv_ref[...])
    kr = kr_ref[...]
    g = g_ref[...]
    lane = lax.broadcasted_iota(jnp.int32, (1, LANES), 1)
    for h in range(MLA_HEADS):
        sl = slice(h * LANES, (h + 1) * LANES)
        kh = kn[:, sl] + kr
        k_ref[h] = (_rms(kh, 1.0 / MLA_QK) * g).astype(BF16)
        ones = (lane == _mla_ones_lane(h)).astype(F32)
        v_ref[h] = (vv[:, sl] + ones).astype(BF16)


def _mla_keys(lat2, kr2, lw):
    t = lat2.shape[0]
    tm = _tile(t, 1024)
    row = lambda i: (i, 0)
    const2 = lambda i: (0, 0)
    hd = lambda i: (0, i, 0)
    shp = jax.ShapeDtypeStruct((MLA_HEADS, t, LANES), BF16)
    return pl.pallas_call(
        _mla_keys_kernel, grid=(t // tm,),
        in_specs=[pl.BlockSpec((tm, MLA_KV_LORA), row), pl.BlockSpec((tm, LANES), row),
                  pl.BlockSpec((MLA_KV_LORA, MLA_HEADS * LANES), const2),
                  pl.BlockSpec((MLA_KV_LORA, MLA_HEADS * LANES), const2),
                  pl.BlockSpec((1, LANES), const2)],
        out_specs=(pl.BlockSpec((MLA_HEADS, tm, LANES), hd), pl.BlockSpec((MLA_HEADS, tm, LANES), hd)),
        out_shape=(shp, shp), compiler_params=_params("parallel"), name="mla_keys",
    )(lat2, kr2, lw["wk"], lw["wv"], lw["g_qk_k"])


def _mla_ones_lane(h):
    return MLA_V if h % 2 == 0 else 0


def _mla_attn_kernel(bound_ref, q_ref, k_ref, v_ref, o_ref, m_ref, acc_ref, *, tq, tk, q_off, kv_len):
    i = pl.program_id(2)
    q0 = q_off + i * tq
    vis_all = jnp.minimum(((q0 >> CHUNK_SHIFT) + 1) << CHUNK_SHIFT, kv_len)
    vis_any = jnp.minimum((((q0 + tq - 1) >> CHUNK_SHIFT) + 1) << CHUNK_SHIFT, kv_len)
    n_full = vis_all // tk
    n_blk = (vis_any + tk - 1) // tk
    c = (MLA_QK ** -0.5) * LOG2_E
    reps = tk // LANES
    bound = bound_ref[0]
    acc_ref[...] = jnp.zeros(acc_ref.shape, F32)

    def step(j, masked, bounded):
        start = pl.multiple_of(j * tk, tk)
        if masked:
            qpos = q0 + lax.broadcasted_iota(jnp.int32, (tq, tk), 0)
            kpos = start + lax.broadcasted_iota(jnp.int32, (tq, tk), 1)
            ok = ((kpos >> CHUNK_SHIFT) <= (qpos >> CHUNK_SHIFT)) & (kpos < kv_len)
        for hh in range(2):
            s = _dot_nt(q_ref[hh], k_ref[hh, pl.ds(start, tk), :])
            if masked:
                s = jnp.where(ok, s, -jnp.inf)
            v = v_ref[hh, pl.ds(start, tk), :]
            if bounded:
                acc_ref[hh] += _dot(jnp.exp2(s * c - bound).astype(BF16), v)
            else:
                m_prev = m_ref[hh]
                m_new = jnp.maximum(m_prev, jnp.max(s, axis=1, keepdims=True))
                alpha = jnp.exp2((m_prev - m_new) * c)
                p = jnp.exp2((s - jnp.tile(m_new, (1, reps))) * c)
                acc_ref[hh] = alpha * acc_ref[hh] + _dot(p.astype(BF16), v)
                m_ref[hh] = m_new

    def loops(bounded):
        lax.fori_loop(0, n_full, lambda j, carry: (step(j, False, bounded), carry)[1], 0)
        lax.fori_loop(n_full, n_blk, lambda j, carry: (step(j, True, bounded), carry)[1], 0)

    @pl.when(bound <= MLA_SAFE_BOUND)
    def _():
        loops(True)

    @pl.when(bound > MLA_SAFE_BOUND)
    def _():
        m_ref[...] = jnp.full(m_ref.shape, NEG_BIG, F32)
        loops(False)

    lane = lax.broadcasted_iota(jnp.int32, (1, LANES), 1)
    outs = []
    for hh in range(2):
        acc = acc_ref[hh]
        denom = jnp.sum(jnp.where(lane == _mla_ones_lane(hh), acc, 0.0), axis=1, keepdims=True)
        outs.append(acc / denom)
    o_ref[...] = jnp.where(lane < MLA_V, outs[0], outs[1]).astype(BF16)


def _mla_attn(bound, q, k, v, batch, lq, lk, q_off, kv_len):
    tq = _tile(lq, 512)
    tk = _tile(lk, ATTN_TILE_ELEMS // tq, LANES)
    assert tk % LANES == 0
    nq = lq // tq
    kern = functools.partial(_mla_attn_kernel, tq=tq, tk=tk, q_off=q_off, kv_len=kv_len)
    return pl.pallas_call(
        kern, grid=(batch, MLA_HEADS // 2, nq),
        in_specs=[pl.BlockSpec(memory_space=pltpu.SMEM),
                  pl.BlockSpec((2, tq, LANES), lambda b, p, i: (p, b * nq + i, 0)),
                  pl.BlockSpec((2, lk, LANES), lambda b, p, i: (p, b, 0)),
                  pl.BlockSpec((2, lk, LANES), lambda b, p, i: (p, b, 0))],
        out_specs=pl.BlockSpec((tq, LANES), lambda b, p, i: (b * nq + i, p)),
        out_shape=jax.ShapeDtypeStruct((batch * lq, MLA_HEADS * MLA_V), BF16),
        scratch_shapes=[pltpu.VMEM((2, tq, LANES), F32), pltpu.VMEM((2, tq, LANES), F32)],
        compiler_params=_params("parallel", "parallel", "arbitrary"), name="mla_attn",
    )(bound, q, k, v)


def _sb_attn_kernel(q_ref, k_ref, v_ref, o_ref, c_ref, acc_ref, *, tq, tk, q_off):
    i = pl.program_id(2)
    q0 = q_off + i * tq
    n_full = q0 // tk
    n_blk = (q0 + tq - 2) // tk + 1
    lane = lax.broadcasted_iota(jnp.int32, (1, LANES), 1)
    upper = (lax.broadcasted_iota(jnp.int32, (SB_SUB, SB_SUB), 0) >
             lax.broadcasted_iota(jnp.int32, (SB_SUB, SB_SUB), 1)).astype(BF16)
    acc_ref[...] = jnp.zeros(acc_ref.shape, F32)
    c_ref[...] = jnp.zeros(c_ref.shape, F32)
    zero = jnp.zeros((), BF16)
    in_head = [(lane >= hh * SB_HEAD_DIM) & (lane < (hh + 1) * SB_HEAD_DIM) for hh in range(2)]
    qs = [jnp.where(in_head[hh], q_ref[...], zero) for hh in range(2)]

    def step(j, masked):
        start = pl.multiple_of(j * tk, tk)
        k = k_ref[pl.ds(start, tk), :]
        v = v_ref[pl.ds(start, tk), :]
        if masked:
            qpos = q0 + lax.broadcasted_iota(jnp.int32, (tq, tk), 0)
            kpos = start + lax.broadcasted_iota(jnp.int32, (tq, tk), 1)
            ok = kpos < qpos
        for hh in range(2):
            z = _dot_nt(qs[hh], k)
            t = jnp.log(1.0 + jnp.exp(-jnp.abs(z)))
            log_beta = jnp.minimum(z, 0.0) - t
            log_keep = log_beta - z
            if masked:
                log_keep = jnp.where(ok, log_keep, 0.0)
            keep16 = log_keep.astype(BF16)
            parts = []
            later = None
            for sb in reversed(range(tk // SB_SUB)):
                sl = slice(sb * SB_SUB, (sb + 1) * SB_SUB)
                inside = _dot(keep16[:, sl], upper)
                if later is not None:
                    inside = inside + later
                parts.append(jnp.exp(log_beta[:, sl] + inside))
                total = jnp.sum(log_keep[:, sl], axis=1, keepdims=True)
                later = total if later is None else later + total
            a = parts[0] if len(parts) == 1 else jnp.concatenate(parts[::-1], axis=1)
            if masked:
                a = jnp.where(ok, a, 0.0)
            before = c_ref[hh]
            acc_ref[...] += jnp.exp(before) * _dot(a.astype(BF16), jnp.where(in_head[hh], v, zero))
            c_ref[hh] = before + later

    def masked_body(it, carry):
        step(n_blk - 1 - it, True)
        return carry

    def full_body(it, carry):
        step(n_full - 1 - it, False)
        return carry

    lax.fori_loop(0, n_blk - n_full, masked_body, 0)
    lax.fori_loop(0, n_full, full_body, 0)
    o_ref[...] = acc_ref[...].astype(BF16)


def _sb_attn(q, k, v, batch, lq, lk, q_off):
    tq = _tile(lq, 512)
    tk = _tile(lk, ATTN_TILE_ELEMS // tq, SB_SUB)
    assert tk % SB_SUB == 0
    nq = lq // tq
    kern = functools.partial(_sb_attn_kernel, tq=tq, tk=tk, q_off=q_off)
    return pl.pallas_call(
        kern, grid=(batch, SB_HEADS // 2, nq),
        in_specs=[pl.BlockSpec((tq, LANES), lambda b, p, i: (b * nq + i, p)),
                  pl.BlockSpec((lk, LANES), lambda b, p, i: (b, p)),
                  pl.BlockSpec((lk, LANES), lambda b, p, i: (b, p))],
        out_specs=pl.BlockSpec((tq, LANES), lambda b, p, i: (b * nq + i, p)),
        out_shape=jax.ShapeDtypeStruct((batch * lq, SB_WIDTH), BF16),
        scratch_shapes=[pltpu.VMEM((2, tq, LANES), F32), pltpu.VMEM((tq, LANES), F32)],
        compiler_params=_params("parallel", "parallel", "arbitrary"), name="sb_attn",
    )(q, k, v)


def _pool_kernel(u_ref, halo_ref, hist_ref, w_ref, sc_ref, y_ref, xb_ref, *, tm, past):
    i = pl.program_id(1)
    xb_ref[POOL_HALO:POOL_HALO + tm, :] = u_ref[...]

    @pl.when(i == 0)
    def _():
        xb_ref[0:POOL_HALO, :] = hist_ref[0]

    @pl.when(i > 0)
    def _():
        xb_ref[0:POOL_HALO, :] = halo_ref[...]

    def back(d):
        return xb_ref[POOL_HALO - d:POOL_HALO - d + tm, :]

    x0 = back(0)
    a2 = x0 + back(1)
    a4 = a2 + back(2) + back(3)
    a8 = a4 + back(4) + back(5) + back(6) + back(7)
    a16 = a8
    for d in range(8, 16):
        a16 = a16 + back(d)
    grp = lax.broadcasted_iota(jnp.int32, (1, POOL_WIDTH), 1) >> 6
    win = jnp.where(grp == 0, a2, jnp.where(grp == 1, a4, jnp.where(grp == 2, a8, a16)))
    width = jnp.where(grp == 0, 2.0, jnp.where(grp == 1, 4.0, jnp.where(grp == 2, 8.0, 16.0)))
    pos = past + i * tm + lax.broadcasted_iota(jnp.int32, (tm, 1), 0)
    cnt = jnp.minimum((pos + 1).astype(F32), width)
    d = win / cnt - x0
    y_ref[...] = (_dot(d.astype(BF16), w_ref[...]) * sc_ref[...]).astype(BF16)


def _pool(u2, hist16, lw, batch, seq_len, past):
    tm = _tile(seq_len, 512)
    nl = seq_len // tm
    per = tm // POOL_HALO
    kern = functools.partial(_pool_kernel, tm=tm, past=past)
    return pl.pallas_call(
        kern, grid=(batch, nl),
        in_specs=[pl.BlockSpec((tm, POOL_WIDTH), lambda b, i: (b * nl + i, 0)),
                  pl.BlockSpec((POOL_HALO, POOL_WIDTH),
                               lambda b, i: (jnp.maximum((b * nl + i) * per - 1, 0), 0)),
                  pl.BlockSpec((1, POOL_HALO, POOL_WIDTH), lambda b, i: (b, 0, 0)),
                  pl.BlockSpec((POOL_WIDTH, POOL_WIDTH), lambda b, i: (0, 0)),
                  pl.BlockSpec((1, POOL_WIDTH), lambda b, i: (0, 0))],
        out_specs=pl.BlockSpec((tm, POOL_WIDTH), lambda b, i: (b * nl + i, 0)),
        out_shape=jax.ShapeDtypeStruct((batch * seq_len, POOL_WIDTH), BF16),
        scratch_shapes=[pltpu.VMEM((POOL_HALO + tm, POOL_WIDTH), F32)],
        compiler_params=_params("parallel", "arbitrary"), name="pool",
    )(u2, u2, hist16, lw["w_pool"], lw["pool_scale"])


def _sigmoid(x):
    return 1.0 / (1.0 + jnp.exp(-x))


def _merge_kernel(x_ref, yp_ref, ym_ref, ys_ref, g1_ref, wg_ref, wbp_ref, wbm_ref, wbs_ref, wo_ref,
                  g2_ref, wr_ref, br_ref, x1_ref, xn2_ref, comb_ref):
    x = x_ref[...]
    xn = (_rms(x) * g1_ref[...]).astype(BF16)
    h = _sigmoid(_dot(xn, wg_ref[:, 0:D_MODEL])) * _dot(yp_ref[...], wbp_ref[...])
    h = h + _sigmoid(_dot(xn, wg_ref[:, D_MODEL:2 * D_MODEL])) * _dot(ym_ref[...], wbm_ref[...])
    h = h + _sigmoid(_dot(xn, wg_ref[:, 2 * D_MODEL:3 * D_MODEL])) * _dot(ys_ref[...], wbs_ref[...])
    x1 = x + _dot(h.astype(BF16), wo_ref[...])
    x1_ref[...] = x1
    xn2 = (_rms(x1) * g2_ref[...]).astype(BF16)
    xn2_ref[...] = xn2
    lg = _dot(xn2, wr_ref[...]) + br_ref[...]
    lane = lax.broadcasted_iota(jnp.int32, lg.shape, 1)
    lanef = lane.astype(F32)
    neg = -jnp.inf
    big = 1e9
    is_g = lane < N_GROUPS
    glog = jnp.where(is_g, lg, neg)
    gmax = jnp.max(glog, axis=1, keepdims=True)
    grp = jnp.min(jnp.where(glog == gmax, lanef, big), axis=1, keepdims=True)
    p_grp = 1.0 / jnp.sum(jnp.where(is_g, jnp.exp(lg - gmax), 0.0), axis=1, keepdims=True)
    eid = lanef - float(ROUTER_EXPERT_LANE0)
    lo = grp * float(EXPERTS_PER_GROUP)
    in_g = (eid >= lo) & (eid < lo + float(EXPERTS_PER_GROUP))
    e1 = jnp.where(in_g, lg, neg)
    v1 = jnp.max(e1, axis=1, keepdims=True)
    i1 = jnp.min(jnp.where(in_g & (e1 == v1), eid, big), axis=1, keepdims=True)
    rest = in_g & (eid != i1)
    e2 = jnp.where(rest, lg, neg)
    v2 = jnp.max(e2, axis=1, keepdims=True)
    i2 = jnp.min(jnp.where(rest & (e2 == v2), eid, big), axis=1, keepdims=True)
    ex = jnp.exp(v2 - v1)
    w1 = (1.0 / (1.0 + ex)) * p_grp
    w2 = (ex / (1.0 + ex)) * p_grp
    comb_ref[...] = jnp.where(eid == i1, w1, 0.0) + jnp.where(eid == i2, w2, 0.0)


def _merge(x2, yp, ym, ys, lw):
    t = x2.shape[0]
    tm = _tile(t, 512)
    row = lambda i: (i, 0)
    const2 = lambda i: (0, 0)
    return pl.pallas_call(
        _merge_kernel, grid=(t // tm,),
        in_specs=[pl.BlockSpec((tm, D_MODEL), row),
                  pl.BlockSpec((tm, POOL_WIDTH), row),
                  pl.BlockSpec((tm, MLA_HEADS * MLA_V), row),
                  pl.BlockSpec((tm, SB_WIDTH), row),
                  pl.BlockSpec((1, D_MODEL), const2),
                  pl.BlockSpec((D_MODEL, 3 * D_MODEL), const2),
                  pl.BlockSpec((POOL_WIDTH, D_MODEL), const2),
                  pl.BlockSpec((MLA_HEADS * MLA_V, D_MODEL), const2),
                  pl.BlockSpec((SB_WIDTH, D_MODEL), const2),
                  pl.BlockSpec((D_MODEL, D_MODEL), const2),
                  pl.BlockSpec((1, D_MODEL), const2),
                  pl.BlockSpec((D_MODEL, LANES), const2),
                  pl.BlockSpec((1, LANES), const2)],
        out_specs=(pl.BlockSpec((tm, D_MODEL), row), pl.BlockSpec((tm, D_MODEL), row),
                   pl.BlockSpec((tm, LANES), row)),
        out_shape=(jax.ShapeDtypeStruct((t, D_MODEL), F32), jax.ShapeDtypeStruct((t, D_MODEL), BF16),
                   jax.ShapeDtypeStruct((t, LANES), F32)),
        compiler_params=_params("parallel"), name="merge",
    )(x2, yp, ym, ys, lw["g_mix"], lw["w_gate"], lw["w_br_pool"], lw["w_br_mla"], lw["w_br_sb"],
      lw["w_out"], lw["g_ffn"], lw["w_router"], lw["b_router"])


def _moe_kernel(xn_ref, comb_ref, x1_ref, wg_ref, wu_ref, wd_ref, o_ref):
    xn = xn_ref[...]
    comb = comb_ref[...]
    group_w = EXPERTS_PER_GROUP * D_EXPERT
    y = x1_ref[...]
    for g in range(N_GROUPS):
        cols = slice(g * group_w, (g + 1) * group_w)
        a = _dot(xn, wg_ref[:, cols])
        h = (a * _sigmoid(a)) * _dot(xn, wu_ref[:, cols])
        scaled = []
        for j in range(EXPERTS_PER_GROUP):
            lane = ROUTER_EXPERT_LANE0 + g * EXPERTS_PER_GROUP + j
            scaled.append((h[:, j * D_EXPERT:(j + 1) * D_EXPERT] * comb[:, lane:lane + 1]).astype(BF16))
        y = y + _dot(jnp.concatenate(scaled, axis=1), wd_ref[cols, :])
    o_ref[...] = y


def _moe(xn2, comb, x1, lw):
    t = xn2.shape[0]
    tm = _tile(t, 512)
    row = lambda i: (i, 0)
    const2 = lambda i: (0, 0)
    hidden = N_EXPERTS * D_EXPERT
    resident = pl.Buffered(1)
    return pl.pallas_call(
        _moe_kernel, grid=(t // tm,),
        in_specs=[pl.BlockSpec((tm, D_MODEL), row), pl.BlockSpec((tm, LANES), row),
                  pl.BlockSpec((tm, D_MODEL), row),
                  pl.BlockSpec((D_MODEL, hidden), const2, pipeline_mode=resident),
                  pl.BlockSpec((D_MODEL, hidden), const2, pipeline_mode=resident),
                  pl.BlockSpec((hidden, D_MODEL), const2, pipeline_mode=resident)],
        out_specs=pl.BlockSpec((tm, D_MODEL), row),
        out_shape=jax.ShapeDtypeStruct((t, D_MODEL), F32),
        compiler_params=_params("parallel"), name="moe",
    )(xn2, comb, x1, lw["w_exp_gate"], lw["w_exp_up"], lw["w_exp_down"])


def _head_blocks(w, n_heads, width, place):
    k = w.shape[0]
    blocks = [place(h, w[:, h * width:(h + 1) * width]) for h in range(n_heads)]
    return jnp.concatenate(blocks, axis=1).reshape(k, n_heads * LANES)


def _pad_lanes(w, before, total=LANES):
    return jnp.pad(w, ((0, 0), (before, total - before - w.shape[1])))


def _swap_halves(w):
    half = w.shape[1] // 2
    return jnp.concatenate([-w[:, half:], w[:, :half]], axis=1)


def _experts_side_by_side(w):
    e, d, f = w.shape
    return jnp.transpose(w.astype(BF16), (1, 0, 2)).reshape(d, e * f)


def _prep_layer(p):
    w_in = p["w_in"]
    w_kr = w_in[:, 640:672]
    w_cat = jnp.concatenate(
        [w_in[:, 0:640], w_in[:, 672:1440],
         _pad_lanes(w_kr, MLA_NOPE), _pad_lanes(_swap_halves(w_kr), MLA_NOPE)], axis=1)
    wq = _head_blocks(p["w_uq"], MLA_HEADS, MLA_QK, lambda h, b: _pad_lanes(b, 0))
    wqs = _head_blocks(p["w_uq"], MLA_HEADS, MLA_QK,
                       lambda h, b: _pad_lanes(_swap_halves(b[:, MLA_NOPE:]), MLA_NOPE))
    wk = _head_blocks(p["w_ukv"], MLA_HEADS, MLA_NOPE + MLA_V,
                      lambda h, b: _pad_lanes(b[:, :MLA_NOPE], 0))
    wv = _head_blocks(p["w_ukv"], MLA_HEADS, MLA_NOPE + MLA_V,
                      lambda h, b: _pad_lanes(b[:, MLA_NOPE:], (h % 2) * MLA_V))
    w_pool = jax.scipy.linalg.block_diag(*[p["w_pool_lin"][g] for g in range(4)])
    w_router = _pad_lanes(jnp.concatenate([p["w_router_group"], p["w_router_expert"]], axis=1), 0)
    b_router = _pad_lanes(jnp.concatenate([p["b_router_group"], p["b_router_expert"]])[None, :], 0)
    return {
        "g_mix": p["g_mix_norm"][None, :],
        "w_cat": w_cat.astype(BF16),
        "g_q_lora": p["g_q_lora"][None, :],
        "wq": wq.astype(BF16), "wqs": wqs.astype(BF16),
        "g_qk_q": _pad_lanes(p["g_qk_q"][None, :], 0),
        "g_kv": p["g_kv_lora"][None, :],
        "wk": wk.astype(BF16), "wv": wv.astype(BF16),
        "g_qk_k": _pad_lanes(p["g_qk_k"][None, :], 0),
        "mla_bound": ((MLA_QK ** 0.5) * LOG2_E * MLA_BOUND_SLACK
                      * jnp.max(jnp.abs(p["g_qk_q"])) * jnp.max(jnp.abs(p["g_qk_k"]))).reshape(1),
        "w_pool": w_pool.astype(BF16),
        "pool_scale": p["pool_scale"][None, :],
        "w_gate": w_in[:, 1440:].astype(BF16),
        "w_br_pool": p["w_br_pool"].astype(BF16),
        "w_br_mla": p["w_br_mla"].astype(BF16),
        "w_br_sb": p["w_br_sb"].astype(BF16),
        "w_out": p["w_out"].astype(BF16),
        "g_ffn": p["g_ffn_norm"][None, :],
        "w_router": w_router.astype(BF16),
        "b_router": b_router,
        "w_exp_gate": _experts_side_by_side(p["w_exp_gate"]),
        "w_exp_up": _experts_side_by_side(p["w_exp_up"]),
        "w_exp_down": p["w_exp_down"].astype(BF16).reshape(N_EXPERTS * D_EXPERT, D_MODEL),
    }


def _rope_table(pos):
    half = MLA_ROPE // 2
    inv = ROPE_THETA ** (-jnp.arange(half, dtype=F32) / half)
    ang = pos.astype(F32)[:, None] * inv[None, :]
    cos, sin = jnp.cos(ang), jnp.sin(ang)
    n = pos.shape[0]
    cos_t = jnp.concatenate([jnp.ones((n, MLA_NOPE), F32), cos, cos, jnp.zeros((n, LANES - MLA_QK), F32)], axis=1)
    sin_t = jnp.concatenate([jnp.zeros((n, MLA_NOPE), F32), sin, sin, jnp.zeros((n, LANES - MLA_QK), F32)], axis=1)
    return jnp.stack([cos_t, sin_t])


def _rows_into(rows, old, new, lane0=0, width=None):
    b, n_old, w_old = old.shape
    width = w_old if width is None else width
    out = jnp.zeros((b, rows, width), new.dtype)
    out = lax.dynamic_update_slice(out, old.astype(new.dtype), (0, 0, lane0))
    return lax.dynamic_update_slice(out, new.reshape(b, -1, new.shape[-1]), (0, n_old, 0))


def _layer(x, cs, lw, hist, past):
    b, seq_len, _ = x.shape
    t = b * seq_len
    x2 = x.reshape(t, D_MODEL)
    up, q, lat, kr, qsb, ksb, vsb, ksbh, vsbh = _inproj(x2, cs, lw, seq_len)
    if hist is None:
        lk = seq_len
        kv_len = seq_len
        lat_all, kr_all, ksb_all, vsb_all = lat, kr, ksbh, vsbh
        hist16 = jnp.zeros((b, POOL_HALO, POOL_WIDTH), F32)
    else:
        kv_len = past + seq_len
        lk = -(-kv_len // 256) * 256
        lat_all = _rows_into(lk, hist["lat"], lat).reshape(b * lk, MLA_KV_LORA)
        kr_all = _rows_into(lk, hist["kr"], kr, MLA_NOPE, LANES).reshape(b * lk, LANES)
        ksb_all = _rows_into(lk, hist["sbk"].reshape(b, past, SB_WIDTH), ksbh).reshape(b * lk, SB_WIDTH)
        vsb_all = _rows_into(lk, hist["sbv"].reshape(b, past, SB_WIDTH), vsbh).reshape(b * lk, SB_WIDTH)
        hist16 = jnp.pad(hist["pool"], ((0, 0), (POOL_HALO - POOL_HIST, 0), (0, 0)))
    k_mla, v_mla = _mla_keys(lat_all, kr_all, lw)
    y_mla = _mla_attn(lw["mla_bound"], q, k_mla, v_mla, b, seq_len, lk, past, kv_len)
    y_sb = _sb_attn(qsb, ksb_all, vsb_all, b, seq_len, lk, past)
    y_pool = _pool(up, hist16, lw, b, seq_len, past)
    x1, xn2, comb = _merge(x2, y_pool, y_mla, y_sb, lw)
    x_out = _moe(xn2, comb, x1, lw).reshape(b, seq_len, D_MODEL)
    up3 = up.reshape(b, seq_len, POOL_WIDTH)
    if seq_len >= POOL_HIST:
        pool_state = up3[:, seq_len - POOL_HIST:]
    else:
        pool_state = jnp.concatenate([hist16[:, 1:], up3], axis=1)[:, -POOL_HIST:]
    state = (lat.reshape(b, seq_len, MLA_KV_LORA),
             kr[:, MLA_NOPE:MLA_QK].reshape(b, seq_len, MLA_ROPE),
             ksb.reshape(b, seq_len, SB_HEADS, SB_HEAD_DIM),
             vsb.reshape(b, seq_len, SB_HEADS, SB_HEAD_DIM),
             pool_state)
    return x_out, state


def kernel(x_prompt, x_sample, cache_mla_latent, cache_mla_krope, cache_sb_k, cache_sb_v, state_pool,
           g_mix_norm, w_in, w_pool_lin, pool_scale, g_q_lora, w_uq, g_kv_lora, w_ukv, g_qk_q, g_qk_k,
           w_br_pool, w_br_mla, w_br_sb, w_out, g_ffn_norm, w_router_group, b_router_group,
           w_router_expert, b_router_expert, w_exp_gate, w_exp_up, w_exp_down):
    depth = w_in.shape[0]
    past = cache_mla_latent.shape[2]
    cs_p = _rope_table(jnp.arange(x_prompt.shape[1], dtype=jnp.int32))
    cs_s = _rope_table(past + jnp.arange(x_sample.shape[1], dtype=jnp.int32))
    xp, xs = x_prompt, x_sample
    new_p = [[] for _ in range(5)]
    new_s = [[] for _ in range(5)]
    for l in range(depth):
        lw = _prep_layer({
            "g_mix_norm": g_mix_norm[l], "w_in": w_in[l], "w_pool_lin": w_pool_lin[l],
            "pool_scale": pool_scale[l], "g_q_lora": g_q_lora[l], "w_uq": w_uq[l],
            "g_kv_lora": g_kv_lora[l], "w_ukv": w_ukv[l], "g_qk_q": g_qk_q[l], "g_qk_k": g_qk_k[l],
            "w_br_pool": w_br_pool[l], "w_br_mla": w_br_mla[l], "w_br_sb": w_br_sb[l],
            "w_out": w_out[l], "g_ffn_norm": g_ffn_norm[l],
            "w_router_group": w_router_group[l], "b_router_group": b_router_group[l],
            "w_router_expert": w_router_expert[l], "b_router_expert": b_router_expert[l],
            "w_exp_gate": w_exp_gate[l], "w_exp_up": w_exp_up[l], "w_exp_down": w_exp_down[l],
        })
        hist = {"lat": cache_mla_latent[l], "kr": cache_mla_krope[l], "sbk": cache_sb_k[l],
                "sbv": cache_sb_v[l], "pool": state_pool[l]}
        xp, st_p = _layer(xp, cs_p, lw, None, 0)
        xs, st_s = _layer(xs, cs_s, lw, hist, past)
        for i in range(5):
            new_p[i].append(st_p[i])
            new_s[i].append(st_s[i])
    return (xp, xs, *[jnp.stack(s) for s in new_p], *[jnp.stack(s) for s in new_s])
```

```python
import functools

import jax
import jax.numpy as jnp
from jax import lax
from jax.experimental import pallas as pl
from jax.experimental.pallas import tpu as pltpu

F32 = jnp.float32
BF16 = jnp.bfloat16

D_MODEL = 1024
CHUNK_SHIFT = 6
EPS = 1e-6
POOL_WIDTH = 256
POOL_HIST = 15
POOL_HALO = 16
POOL_TOP = 32
MLA_HEADS = 8
MLA_Q_LORA = 256
MLA_KV_LORA = 128
MLA_NOPE = 64
MLA_ROPE = 32
MLA_V = 64
MLA_QK = MLA_NOPE + MLA_ROPE
ROPE_THETA = 10000.0
SB_HEADS = 4
SB_HEAD_DIM = 64
SB_WIDTH = 256
N_GROUPS = 4
EXPERTS_PER_GROUP = 4
N_EXPERTS = 16
D_EXPERT = 256
LANES = 128
ROUTER_EXPERT_LANE0 = N_GROUPS
VMEM_LIMIT = 56 * 1024 * 1024
NEG_BIG = -1e30
LOG2_E = 1.4426950408889634
SB_SUB = 256
ATTN_TILE_ELEMS = 512 * 512
MLA_STEP_HEADS = 8
MLA_BOUND_SLACK = 1.02
MLA_SAFE_BOUND = 60.0

C_UPOOL = 0
C_CQ = 256
C_CKV = 512
C_QSB = 640
C_KSB = 896
C_VSB = 1152
C_KRA = 1408
C_KRB = 1536
C_TOTAL = 1664


def _tile(n, pref, mult=16):
    best = None
    for t in range(mult, min(n, pref) + 1, mult):
        if n % t == 0:
            best = t
    return best if best is not None else n


def _params(*sem, flags=None):
    return pltpu.CompilerParams(dimension_semantics=sem, vmem_limit_bytes=VMEM_LIMIT, flags=flags)


def _rms(x, inv_n=None):
    if inv_n is None:
        ms = jnp.mean(x * x, axis=-1, keepdims=True)
    else:
        ms = jnp.sum(x * x, axis=-1, keepdims=True) * inv_n
    return x * lax.rsqrt(ms + EPS)


def _dot(a, b):
    return jnp.dot(a, b, preferred_element_type=F32)


def _dot_nt(a, b):
    return lax.dot_general(a, b, (((1,), (1,)), ((), ())), preferred_element_type=F32)


def _inproj_kernel(x_ref, g_ref, w_ref, cs_ref, gq_ref, wq_ref, wqs_ref, gqk_ref, gkv_ref,
                   up_ref, q_ref, lat_ref, kr_ref, qsb_ref, ksb_ref, vsb_ref, ksbh_ref, vsbh_ref):
    x = x_ref[...]
    xn = (_rms(x) * g_ref[...]).astype(BF16)
    z = _dot(xn, w_ref[...])
    up_ref[...] = z[:, C_UPOOL:C_UPOOL + POOL_WIDTH]
    cq = z[:, C_CQ:C_CQ + MLA_Q_LORA]
    cqn = (_rms(cq) * gq_ref[...]).astype(BF16)
    qa = _dot(cqn, wq_ref[...])
    qb = _dot(cqn, wqs_ref[...])
    cos = cs_ref[0]
    sin = cs_ref[1]
    gqk = gqk_ref[...]
    for h in range(MLA_HEADS):
        sl = slice(h * LANES, (h + 1) * LANES)
        qh = qa[:, sl] * cos + qb[:, sl] * sin
        q_ref[h] = (_rms(qh, 1.0 / MLA_QK) * gqk).astype(BF16)
    ckv = z[:, C_CKV:C_CKV + MLA_KV_LORA]
    lat_ref[...] = _rms(ckv) * gkv_ref[...]
    kr_ref[...] = z[:, C_KRA:C_KRA + LANES] * cos + z[:, C_KRB:C_KRB + LANES] * sin
    qsb_ref[...] = (z[:, C_QSB:C_QSB + SB_WIDTH] * (SB_HEAD_DIM ** -0.5)).astype(BF16)
    ksb = z[:, C_KSB:C_KSB + SB_WIDTH]
    ksb_ref[...] = ksb
    ksbh_ref[...] = ksb.astype(BF16)
    vsb = z[:, C_VSB:C_VSB + SB_WIDTH]
    vsb_ref[...] = vsb
    vsbh_ref[...] = vsb.astype(BF16)


def _inproj(x2, cs, lw, seq_len):
    t = x2.shape[0]
    tm = _tile(t, 512)
    if tm <= seq_len and seq_len % tm == 0:
        per = seq_len // tm
        cs_map = lambda i: (0, i % per, 0)
    else:
        assert tm % seq_len == 0
        cs = jnp.tile(cs, (1, tm // seq_len, 1))
        cs_map = lambda i: (0, 0, 0)
    row = lambda i: (i, 0)
    const2 = lambda i: (0, 0)
    out_shape = (
        jax.ShapeDtypeStruct((t, POOL_WIDTH), F32),
        jax.ShapeDtypeStruct((MLA_HEADS, t, LANES), BF16),
        jax.ShapeDtypeStruct((t, MLA_KV_LORA), F32),
        jax.ShapeDtypeStruct((t, LANES), F32),
        jax.ShapeDtypeStruct((t, SB_WIDTH), BF16),
        jax.ShapeDtypeStruct((t, SB_WIDTH), F32),
        jax.ShapeDtypeStruct((t, SB_WIDTH), F32),
        jax.ShapeDtypeStruct((t, SB_WIDTH), BF16),
        jax.ShapeDtypeStruct((t, SB_WIDTH), BF16),
    )
    out_specs = (
        pl.BlockSpec((tm, POOL_WIDTH), row),
        pl.BlockSpec((MLA_HEADS, tm, LANES), lambda i: (0, i, 0)),
        pl.BlockSpec((tm, MLA_KV_LORA), row),
        pl.BlockSpec((tm, LANES), row),
        pl.BlockSpec((tm, SB_WIDTH), row),
        pl.BlockSpec((tm, SB_WIDTH), row),
        pl.BlockSpec((tm, SB_WIDTH), row),
        pl.BlockSpec((tm, SB_WIDTH), row),
        pl.BlockSpec((tm, SB_WIDTH), row),
    )
    in_specs = [
        pl.BlockSpec((tm, D_MODEL), row),
        pl.BlockSpec((1, D_MODEL), const2),
        pl.BlockSpec((D_MODEL, C_TOTAL), const2),
        pl.BlockSpec((2, tm, LANES), cs_map),
        pl.BlockSpec((1, MLA_Q_LORA), const2),
        pl.BlockSpec((MLA_Q_LORA, MLA_HEADS * LANES), const2),
        pl.BlockSpec((MLA_Q_LORA, MLA_HEADS * LANES), const2),
        pl.BlockSpec((1, LANES), const2),
        pl.BlockSpec((1, MLA_KV_LORA), const2),
    ]
    return pl.pallas_call(
        _inproj_kernel, grid=(t // tm,), in_specs=in_specs, out_specs=out_specs, out_shape=out_shape,
        compiler_params=_params("parallel"), name="inproj",
    )(x2, lw["g_mix"], lw["w_cat"], cs, lw["g_q_lora"], lw["wq"], lw["wqs"], lw["g_qk_q"], lw["g_kv"])


def _mla_keys_kernel(lat_ref, kr_ref, wk_ref, wv_ref, g_ref, k_ref, v_ref):
    latb = lat_ref[...].astype(BF16)
    kn = _dot(latb, wk_ref[...])
    vv = _dot(latb, wv_ref[...])
    kr = kr_ref[...]
    g = g_ref[...]
    lane = lax.broadcasted_iota(jnp.int32, (1, LANES), 1)
    for h in range(MLA_HEADS):
        sl = slice(h * LANES, (h + 1) * LANES)
        kh = kn[:, sl] + kr
        k_ref[h] = (_rms(kh, 1.0 / MLA_QK) * g).astype(BF16)
        ones = (lane == _mla_ones_lane(h)).astype(F32)
        v_ref[h] = (vv[:, sl] + ones).astype(BF16)


def _mla_keys(lat2, kr2, lw):
    t = lat2.shape[0]
    tm = _tile(t, 1024)
    row = lambda i: (i, 0)
    const2 = lambda i: (0, 0)
    hd = lambda i: (0, i, 0)
    shp = jax.ShapeDtypeStruct((MLA_HEADS, t, LANES), BF16)
    return pl.pallas_call(
        _mla_keys_kernel, grid=(t // tm,),
        in_specs=[pl.BlockSpec((tm, MLA_KV_LORA), row), pl.BlockSpec((tm, LANES), row),
                  pl.BlockSpec((MLA_KV_LORA, MLA_HEADS * LANES), const2),
                  pl.BlockSpec((MLA_KV_LORA, MLA_HEADS * LANES), const2),
                  pl.BlockSpec((1, LANES), const2)],
        out_specs=(pl.BlockSpec((MLA_HEADS, tm, LANES), hd), pl.BlockSpec((MLA_HEADS, tm, LANES), hd)),
        out_shape=(shp, shp), compiler_params=_params("parallel"), name="mla_keys",
    )(lat2, kr2, lw["wk"], lw["wv"], lw["g_qk_k"])


def _mla_ones_lane(h):
    return MLA_V if h % 2 == 0 else 0


def _mla_attn_kernel(bound_ref, q_ref, k_ref, v_ref, o_ref, m_ref, acc_ref, *, tq, tk, q_off, kv_len):
    i = pl.program_id(2)
    q0 = q_off + i * tq
    vis_all = jnp.minimum(((q0 >> CHUNK_SHIFT) + 1) << CHUNK_SHIFT, kv_len)
    vis_any = jnp.minimum((((q0 + tq - 1) >> CHUNK_SHIFT) + 1) << CHUNK_SHIFT, kv_len)
    n_full = vis_all // tk
    n_blk = (vis_any + tk - 1) // tk
    c = (MLA_QK ** -0.5) * LOG2_E
    reps = tk // LANES
    bound = bound_ref[0]
    acc_ref[...] = jnp.zeros(acc_ref.shape, F32)

    def step(j, masked, bounded):
        start = pl.multiple_of(j * tk, tk)
        if masked:
            qpos = q0 + lax.broadcasted_iota(jnp.int32, (tq, tk), 0)
            kpos = start + lax.broadcasted_iota(jnp.int32, (tq, tk), 1)
            ok = ((kpos >> CHUNK_SHIFT) <= (qpos >> CHUNK_SHIFT)) & (kpos < kv_len)
        for hh in range(MLA_STEP_HEADS):
            s = _dot_nt(q_ref[hh], k_ref[hh, pl.ds(start, tk), :])
            if masked:
                s = jnp.where(ok, s, -jnp.inf)
            v = v_ref[hh, pl.ds(start, tk), :]
            if bounded:
                acc_ref[hh] += _dot(jnp.exp2(s * c - bound).astype(BF16), v)
            else:
                m_prev = m_ref[hh]
                m_new = jnp.maximum(m_prev, jnp.max(s, axis=1, keepdims=True))
                alpha = jnp.exp2((m_prev - m_new) * c)
                p = jnp.exp2((s - jnp.tile(m_new, (1, reps))) * c)
                acc_ref[hh] = alpha * acc_ref[hh] + _dot(p.astype(BF16), v)
                m_ref[hh] = m_new

    def loops(bounded):
        lax.fori_loop(0, n_full, lambda j, carry: (step(j, False, bounded), carry)[1], 0)
        lax.fori_loop(n_full, n_blk, lambda j, carry: (step(j, True, bounded), carry)[1], 0)

    @pl.when(bound <= MLA_SAFE_BOUND)
    def _():
        loops(True)

    @pl.when(bound > MLA_SAFE_BOUND)
    def _():
        m_ref[...] = jnp.full(m_ref.shape, NEG_BIG, F32)
        loops(False)

    lane = lax.broadcasted_iota(jnp.int32, (1, LANES), 1)
    for pair in range(MLA_STEP_HEADS // 2):
        outs = []
        for hh in (2 * pair, 2 * pair + 1):
            acc = acc_ref[hh]
            denom = jnp.sum(jnp.where(lane == _mla_ones_lane(hh), acc, 0.0), axis=1, keepdims=True)
            outs.append(acc / denom)
        o_ref[:, pair * LANES:(pair + 1) * LANES] = jnp.where(lane < MLA_V, outs[0], outs[1]).astype(BF16)


def _mla_attn(bound, q, k, v, batch, lq, lk, q_off, kv_len):
    tq = _tile(lq, 512)
    tk = _tile(lk, ATTN_TILE_ELEMS // tq, LANES)
    assert tk % LANES == 0
    nq = lq // tq
    nh = MLA_STEP_HEADS
    kern = functools.partial(_mla_attn_kernel, tq=tq, tk=tk, q_off=q_off, kv_len=kv_len)
    return pl.pallas_call(
        kern, grid=(batch, MLA_HEADS // nh, nq),
        in_specs=[pl.BlockSpec(memory_space=pltpu.SMEM),
                  pl.BlockSpec((nh, tq, LANES), lambda b, p, i: (p, b * nq + i, 0)),
                  pl.BlockSpec((nh, lk, LANES), lambda b, p, i: (p, b, 0)),
                  pl.BlockSpec((nh, lk, LANES), lambda b, p, i: (p, b, 0))],
        out_specs=pl.BlockSpec((tq, nh * MLA_V), lambda b, p, i: (b * nq + i, p)),
        out_shape=jax.ShapeDtypeStruct((batch * lq, MLA_HEADS * MLA_V), BF16),
        scratch_shapes=[pltpu.VMEM((nh, tq, LANES), F32), pltpu.VMEM((nh, tq, LANES), F32)],
        compiler_params=_params("parallel", "parallel", "arbitrary"), name="mla_attn",
    )(bound, q, k, v)


def _sb_attn_kernel(q_ref, k_ref, v_ref, o_ref, c_ref, acc_ref, *, tq, tk, q_off):
    i = pl.program_id(2)
    q0 = q_off + i * tq
    n_full = q0 // tk
    n_blk = (q0 + tq - 2) // tk + 1
    lane = lax.broadcasted_iota(jnp.int32, (1, LANES), 1)
    upper = (lax.broadcasted_iota(jnp.int32, (SB_SUB, SB_SUB), 0) >
             lax.broadcasted_iota(jnp.int32, (SB_SUB, SB_SUB), 1)).astype(BF16)
    acc_ref[...] = jnp.zeros(acc_ref.shape, F32)
    c_ref[...] = jnp.zeros(c_ref.shape, F32)
    zero = jnp.zeros((), BF16)
    pair_lanes = [slice((hh // 2) * LANES, (hh // 2 + 1) * LANES) for hh in range(SB_HEADS)]
    in_head = [(lane >= (hh % 2) * SB_HEAD_DIM) & (lane < (hh % 2 + 1) * SB_HEAD_DIM) for hh in range(SB_HEADS)]
    qs = [jnp.where(in_head[hh], q_ref[:, pair_lanes[hh]], zero) for hh in range(SB_HEADS)]

    def step(j, masked):
        start = pl.multiple_of(j * tk, tk)
        if masked:
            qpos = q0 + lax.broadcasted_iota(jnp.int32, (tq, tk), 0)
            kpos = start + lax.broadcasted_iota(jnp.int32, (tq, tk), 1)
            ok = kpos < qpos
        for hh in range(SB_HEADS):
            k = k_ref[pl.ds(start, tk), pair_lanes[hh]]
            v = v_ref[pl.ds(start, tk), pair_lanes[hh]]
            z = _dot_nt(qs[hh], k)
            t = jnp.log(1.0 + jnp.exp(-jnp.abs(z)))
            log_beta = jnp.minimum(z, 0.0) - t
            log_keep = log_beta - z
            if masked:
                log_keep = jnp.where(ok, log_keep, 0.0)
            keep16 = log_keep.astype(BF16)
            parts = []
            later = None
            for sb in reversed(range(tk // SB_SUB)):
                sl = slice(sb * SB_SUB, (sb + 1) * SB_SUB)
                inside = _dot(keep16[:, sl], upper)
                if later is not None:
                    inside = inside + later
                parts.append(jnp.exp(log_beta[:, sl] + inside))
                total = jnp.sum(log_keep[:, sl], axis=1, keepdims=True)
                later = total if later is None else later + total
            a = parts[0] if len(parts) == 1 else jnp.concatenate(parts[::-1], axis=1)
            if masked:
                a = jnp.where(ok, a, 0.0)
            before = c_ref[hh]
            pv = _dot(a.astype(BF16), jnp.where(in_head[hh], v, zero))
            acc_ref[:, pair_lanes[hh]] += jnp.exp(before) * pv
            c_ref[hh] = before + later

    def masked_body(it, carry):
        step(n_blk - 1 - it, True)
        return carry

    def full_body(it, carry):
        step(n_full - 1 - it, False)
        return carry

    lax.fori_loop(0, n_blk - n_full, masked_body, 0)
    lax.fori_loop(0, n_full, full_body, 0)
    o_ref[...] = acc_ref[...].astype(BF16)


def _sb_attn(q, k, v, batch, lq, lk, q_off):
    tq = _tile(lq, 512)
    tk = _tile(lk, ATTN_TILE_ELEMS // tq, SB_SUB)
    assert tk % SB_SUB == 0
    nq = lq // tq
    kern = functools.partial(_sb_attn_kernel, tq=tq, tk=tk, q_off=q_off)
    return pl.pallas_call(
        kern, grid=(batch, 1, nq),
        in_specs=[pl.BlockSpec((tq, SB_WIDTH), lambda b, p, i: (b * nq + i, 0)),
                  pl.BlockSpec((lk, SB_WIDTH), lambda b, p, i: (b, 0)),
                  pl.BlockSpec((lk, SB_WIDTH), lambda b, p, i: (b, 0))],
        out_specs=pl.BlockSpec((tq, SB_WIDTH), lambda b, p, i: (b * nq + i, 0)),
        out_shape=jax.ShapeDtypeStruct((batch * lq, SB_WIDTH), BF16),
        scratch_shapes=[pltpu.VMEM((SB_HEADS, tq, LANES), F32), pltpu.VMEM((tq, SB_WIDTH), F32)],
        compiler_params=_params("parallel", "parallel", "arbitrary"), name="sb_attn",
    )(q, k, v)


def _pool_kernel(u_ref, halo_ref, hist_ref, w_ref, sc_ref, y_ref, s0, s1, s2, s3, *, tm, past):
    i = pl.program_id(1)
    top = POOL_TOP
    end = top + tm
    s0[0:top - POOL_HALO, :] = jnp.zeros((top - POOL_HALO, POOL_WIDTH), F32)
    s0[top:end, :] = u_ref[...]

    @pl.when(i == 0)
    def _():
        s0[top - POOL_HALO:top, :] = hist_ref[0]

    @pl.when(i > 0)
    def _():
        s0[top - POOL_HALO:top, :] = halo_ref[...]

    s1[8:end, :] = s0[8:end, :] + s0[7:end - 1, :]
    s2[16:end, :] = s1[16:end, :] + s1[14:end - 2, :]
    s3[24:end, :] = s2[24:end, :] + s2[20:end - 4, :]
    x0 = s0[top:end, :]
    a2 = s1[top:end, :]
    a4 = s2[top:end, :]
    a8 = s3[top:end, :]
    a16 = a8 + s3[top - 8:end - 8, :]
    grp = lax.broadcasted_iota(jnp.int32, (1, POOL_WIDTH), 1) >> 6
    win = jnp.where(grp == 0, a2, jnp.where(grp == 1, a4, jnp.where(grp == 2, a8, a16)))
    width = jnp.where(grp == 0, 2.0, jnp.where(grp == 1, 4.0, jnp.where(grp == 2, 8.0, 16.0)))
    pos = past + i * tm + lax.broadcasted_iota(jnp.int32, (tm, 1), 0)
    cnt = jnp.minimum((pos + 1).astype(F32), width)
    d = win / cnt - x0
    y_ref[...] = (_dot(d.astype(BF16), w_ref[...]) * sc_ref[...]).astype(BF16)


def _pool(u2, hist16, lw, batch, seq_len, past):
    tm = _tile(seq_len, 512)
    nl = seq_len // tm
    per = tm // POOL_HALO
    kern = functools.partial(_pool_kernel, tm=tm, past=past)
    return pl.pallas_call(
        kern, grid=(batch, nl),
        in_specs=[pl.BlockSpec((tm, POOL_WIDTH), lambda b, i: (b * nl + i, 0)),
                  pl.BlockSpec((POOL_HALO, POOL_WIDTH),
                               lambda b, i: (jnp.maximum((b * nl + i) * per - 1, 0), 0)),
                  pl.BlockSpec((1, POOL_HALO, POOL_WIDTH), lambda b, i: (b, 0, 0)),
                  pl.BlockSpec((POOL_WIDTH, POOL_WIDTH), lambda b, i: (0, 0)),
                  pl.BlockSpec((1, POOL_WIDTH), lambda b, i: (0, 0))],
        out_specs=pl.BlockSpec((tm, POOL_WIDTH), lambda b, i: (b * nl + i, 0)),
        out_shape=jax.ShapeDtypeStruct((batch * seq_len, POOL_WIDTH), BF16),
        scratch_shapes=[pltpu.VMEM((POOL_TOP + tm, POOL_WIDTH), F32)] * 4,
        compiler_params=_params("parallel", "arbitrary"), name="pool",
    )(u2, u2, hist16, lw["w_pool"], lw["pool_scale"])


def _sigmoid(x):
    return 1.0 / (1.0 + jnp.exp(-x))


def _merge_kernel(x_ref, yp_ref, ym_ref, ys_ref, g1_ref, wg_ref, wbp_ref, wbm_ref, wbs_ref, wo_ref,
                  g2_ref, wr_ref, br_ref, x1_ref, xn2_ref, comb_ref):
    x = x_ref[...]
    xn = (_rms(x) * g1_ref[...]).astype(BF16)
    h = _sigmoid(_dot(xn, wg_ref[:, 0:D_MODEL])) * _dot(yp_ref[...], wbp_ref[...])
    h = h + _sigmoid(_dot(xn, wg_ref[:, D_MODEL:2 * D_MODEL])) * _dot(ym_ref[...], wbm_ref[...])
    h = h + _sigmoid(_dot(xn, wg_ref[:, 2 * D_MODEL:3 * D_MODEL])) * _dot(ys_ref[...], wbs_ref[...])
    x1 = x + _dot(h.astype(BF16), wo_ref[...])
    x1_ref[...] = x1
    xn2 = (_rms(x1) * g2_ref[...]).astype(BF16)
    xn2_ref[...] = xn2
    lg = _dot(xn2, wr_ref[...]) + br_ref[...]
    lane = lax.broadcasted_iota(jnp.int32, lg.shape, 1)
    lanef = lane.astype(F32)
    neg = -jnp.inf
    big = 1e9
    is_g = lane < N_GROUPS
    glog = jnp.where(is_g, lg, neg)
    gmax = jnp.max(glog, axis=1, keepdims=True)
    grp = jnp.min(jnp.where(glog == gmax, lanef, big), axis=1, keepdims=True)
    p_grp = 1.0 / jnp.sum(jnp.where(is_g, jnp.exp(lg - gmax), 0.0), axis=1, keepdims=True)
    eid = lanef - float(ROUTER_EXPERT_LANE0)
    lo = grp * float(EXPERTS_PER_GROUP)
    in_g = (eid >= lo) & (eid < lo + float(EXPERTS_PER_GROUP))
    e1 = jnp.where(in_g, lg, neg)
    v1 = jnp.max(e1, axis=1, keepdims=True)
    i1 = jnp.min(jnp.where(in_g & (e1 == v1), eid, big), axis=1, keepdims=True)
    rest = in_g & (eid != i1)
    e2 = jnp.where(rest, lg, neg)
    v2 = jnp.max(e2, axis=1, keepdims=True)
    i2 = jnp.min(jnp.where(rest & (e2 == v2), eid, big), axis=1, keepdims=True)
    ex = jnp.exp(v2 - v1)
    w1 = (1.0 / (1.0 + ex)) * p_grp
    w2 = (ex / (1.0 + ex)) * p_grp
    comb_ref[...] = jnp.where(eid == i1, w1, 0.0) + jnp.where(eid == i2, w2, 0.0)


def _merge(x2, yp, ym, ys, lw):
    t = x2.shape[0]
    tm = _tile(t, 512)
    row = lambda i: (i, 0)
    const2 = lambda i: (0, 0)
    return pl.pallas_call(
        _merge_kernel, grid=(t // tm,),
        in_specs=[pl.BlockSpec((tm, D_MODEL), row),
                  pl.BlockSpec((tm, POOL_WIDTH), row),
                  pl.BlockSpec((tm, MLA_HEADS * MLA_V), row),
                  pl.BlockSpec((tm, SB_WIDTH), row),
                  pl.BlockSpec((1, D_MODEL), const2),
                  pl.BlockSpec((D_MODEL, 3 * D_MODEL), const2),
                  pl.BlockSpec((POOL_WIDTH, D_MODEL), const2),
                  pl.BlockSpec((MLA_HEADS * MLA_V, D_MODEL), const2),
                  pl.BlockSpec((SB_WIDTH, D_MODEL), const2),
                  pl.BlockSpec((D_MODEL, D_MODEL), const2),
                  pl.BlockSpec((1, D_MODEL), const2),
                  pl.BlockSpec((D_MODEL, LANES), const2),
                  pl.BlockSpec((1, LANES), const2)],
        out_specs=(pl.BlockSpec((tm, D_MODEL), row), pl.BlockSpec((tm, D_MODEL), row),
                   pl.BlockSpec((tm, LANES), row)),
        out_shape=(jax.ShapeDtypeStruct((t, D_MODEL), F32), jax.ShapeDtypeStruct((t, D_MODEL), BF16),
                   jax.ShapeDtypeStruct((t, LANES), F32)),
        compiler_params=_params("parallel"), name="merge",
    )(x2, yp, ym, ys, lw["g_mix"], lw["w_gate"], lw["w_br_pool"], lw["w_br_mla"], lw["w_br_sb"],
      lw["w_out"], lw["g_ffn"], lw["w_router"], lw["b_router"])


def _moe_kernel(xn_ref, comb_ref, x1_ref, wg_ref, wu_ref, wd_ref, o_ref):
    xn = xn_ref[...]
    comb = comb_ref[...]
    group_w = EXPERTS_PER_GROUP * D_EXPERT
    y = x1_ref[...]
    for g in range(N_GROUPS):
        cols = slice(g * group_w, (g + 1) * group_w)
        a = _dot(xn, wg_ref[:, cols])
        h = (a * _sigmoid(a)) * _dot(xn, wu_ref[:, cols])
        scaled = []
        for j in range(EXPERTS_PER_GROUP):
            lane = ROUTER_EXPERT_LANE0 + g * EXPERTS_PER_GROUP + j
            scaled.append((h[:, j * D_EXPERT:(j + 1) * D_EXPERT] * comb[:, lane:lane + 1]).astype(BF16))
        y = y + _dot(jnp.concatenate(scaled, axis=1), wd_ref[cols, :])
    o_ref[...] = y


def _moe(xn2, comb, x1, lw):
    t = xn2.shape[0]
    tm = _tile(t, 512)
    row = lambda i: (i, 0)
    const2 = lambda i: (0, 0)
    hidden = N_EXPERTS * D_EXPERT
    resident = pl.Buffered(1)
    return pl.pallas_call(
        _moe_kernel, grid=(t // tm,),
        in_specs=[pl.BlockSpec((tm, D_MODEL), row), pl.BlockSpec((tm, LANES), row),
                  pl.BlockSpec((tm, D_MODEL), row),
                  pl.BlockSpec((D_MODEL, hidden), const2, pipeline_mode=resident),
                  pl.BlockSpec((D_MODEL, hidden), const2, pipeline_mode=resident),
                  pl.BlockSpec((hidden, D_MODEL), const2, pipeline_mode=resident)],
        out_specs=pl.BlockSpec((tm, D_MODEL), row),
        out_shape=jax.ShapeDtypeStruct((t, D_MODEL), F32),
        compiler_params=_params("parallel"), name="moe",
    )(xn2, comb, x1, lw["w_exp_gate"], lw["w_exp_up"], lw["w_exp_down"])


def _head_blocks(w, n_heads, width, place):
    k = w.shape[0]
    blocks = [place(h, w[:, h * width:(h + 1) * width]) for h in range(n_heads)]
    return jnp.concatenate(blocks, axis=1).reshape(k, n_heads * LANES)


def _pad_lanes(w, before, total=LANES):
    return jnp.pad(w, ((0, 0), (before, total - before - w.shape[1])))


def _swap_halves(w):
    half = w.shape[1] // 2
    return jnp.concatenate([-w[:, half:], w[:, :half]], axis=1)


def _experts_side_by_side(w):
    e, d, f = w.shape
    return jnp.transpose(w.astype(BF16), (1, 0, 2)).reshape(d, e * f)


def _prep_layer(p):
    w_in = p["w_in"].astype(BF16)
    w_uq = p["w_uq"].astype(BF16)
    w_ukv = p["w_ukv"].astype(BF16)
    w_kr = w_in[:, 640:672]
    w_cat = jnp.concatenate(
        [w_in[:, 0:640], w_in[:, 672:1440],
         _pad_lanes(w_kr, MLA_NOPE), _pad_lanes(_swap_halves(w_kr), MLA_NOPE)], axis=1)
    wq = _head_blocks(w_uq, MLA_HEADS, MLA_QK, lambda h, b: _pad_lanes(b, 0))
    wqs = _head_blocks(w_uq, MLA_HEADS, MLA_QK,
                       lambda h, b: _pad_lanes(_swap_halves(b[:, MLA_NOPE:]), MLA_NOPE))
    wk = _head_blocks(w_ukv, MLA_HEADS, MLA_NOPE + MLA_V,
                      lambda h, b: _pad_lanes(b[:, :MLA_NOPE], 0))
    wv = _head_blocks(w_ukv, MLA_HEADS, MLA_NOPE + MLA_V,
                      lambda h, b: _pad_lanes(b[:, MLA_NOPE:], (h % 2) * MLA_V))
    w_pool = jax.scipy.linalg.block_diag(*[p["w_pool_lin"][g] for g in range(4)])
    w_router = _pad_lanes(jnp.concatenate([p["w_router_group"], p["w_router_expert"]], axis=1), 0)
    b_router = _pad_lanes(jnp.concatenate([p["b_router_group"], p["b_router_expert"]])[None, :], 0)
    return {
        "g_mix": p["g_mix_norm"][None, :],
        "w_cat": w_cat.astype(BF16),
        "g_q_lora": p["g_q_lora"][None, :],
        "wq": wq.astype(BF16), "wqs": wqs.astype(BF16),
        "g_qk_q": _pad_lanes(p["g_qk_q"][None, :], 0),
        "g_kv": p["g_kv_lora"][None, :],
        "wk": wk.astype(BF16), "wv": wv.astype(BF16),
        "g_qk_k": _pad_lanes(p["g_qk_k"][None, :], 0),
        "mla_bound": ((MLA_QK ** 0.5) * LOG2_E * MLA_BOUND_SLACK
                      * jnp.max(jnp.abs(p["g_qk_q"])) * jnp.max(jnp.abs(p["g_qk_k"]))).reshape(1),
        "w_pool": w_pool.astype(BF16),
        "pool_scale": p["pool_scale"][None, :],
        "w_gate": w_in[:, 1440:].astype(BF16),
        "w_br_pool": p["w_br_pool"].astype(BF16),
        "w_br_mla": p["w_br_mla"].astype(BF16),
        "w_br_sb": p["w_br_sb"].astype(BF16),
        "w_out": p["w_out"].astype(BF16),
        "g_ffn": p["g_ffn_norm"][None, :],
        "w_router": w_router.astype(BF16),
        "b_router": b_router,
        "w_exp_gate": _experts_side_by_side(p["w_exp_gate"]),
        "w_exp_up": _experts_side_by_side(p["w_exp_up"]),
        "w_exp_down": p["w_exp_down"].astype(BF16).reshape(N_EXPERTS * D_EXPERT, D_MODEL),
    }


def _rope_table(pos):
    half = MLA_ROPE // 2
    inv = ROPE_THETA ** (-jnp.arange(half, dtype=F32) / half)
    ang = pos.astype(F32)[:, None] * inv[None, :]
    cos, sin = jnp.cos(ang), jnp.sin(ang)
    n = pos.shape[0]
    cos_t = jnp.concatenate([jnp.ones((n, MLA_NOPE), F32), cos, cos, jnp.zeros((n, LANES - MLA_QK), F32)], axis=1)
    sin_t = jnp.concatenate([jnp.zeros((n, MLA_NOPE), F32), sin, sin, jnp.zeros((n, LANES - MLA_QK), F32)], axis=1)
    return jnp.stack([cos_t, sin_t])


def _rows_into(rows, old, new, lane0=0, width=None):
    b, n_old, w_old = old.shape
    width = w_old if width is None else width
    old = jnp.pad(old.astype(new.dtype), ((0, 0), (0, 0), (lane0, width - lane0 - w_old)))
    new = new.reshape(b, -1, width)
    tail = jnp.zeros((b, rows - n_old - new.shape[1], width), new.dtype)
    return jnp.concatenate([old, new, tail], axis=1)


def _layer(x, cs, lw, hist, past):
    b, seq_len, _ = x.shape
    t = b * seq_len
    x2 = x.reshape(t, D_MODEL)
    up, q, lat, kr, qsb, ksb, vsb, ksbh, vsbh = _inproj(x2, cs, lw, seq_len)
    if hist is None:
        lk = seq_len
        kv_len = seq_len
        lat_all, kr_all, ksb_all, vsb_all = lat, kr, ksbh, vsbh
        hist16 = jnp.zeros((b, POOL_HALO, POOL_WIDTH), F32)
    else:
        kv_len = past + seq_len
        lk = -(-kv_len // 256) * 256
        lat_all = _rows_into(lk, hist["lat"], lat).reshape(b * lk, MLA_KV_LORA)
        kr_all = _rows_into(lk, hist["kr"], kr, MLA_NOPE, LANES).reshape(b * lk, LANES)
        ksb_all = _rows_into(lk, hist["sbk"].reshape(b, past, SB_WIDTH), ksbh).reshape(b * lk, SB_WIDTH)
        vsb_all = _rows_into(lk, hist["sbv"].reshape(b, past, SB_WIDTH), vsbh).reshape(b * lk, SB_WIDTH)
        hist16 = jnp.pad(hist["pool"], ((0, 0), (POOL_HALO - POOL_HIST, 0), (0, 0)))
    k_mla, v_mla = _mla_keys(lat_all, kr_all, lw)
    y_mla = _mla_attn(lw["mla_bound"], q, k_mla, v_mla, b, seq_len, lk, past, kv_len)
    y_sb = _sb_attn(qsb, ksb_all, vsb_all, b, seq_len, lk, past)
    y_pool = _pool(up, hist16, lw, b, seq_len, past)
    x1, xn2, comb = _merge(x2, y_pool, y_mla, y_sb, lw)
    x_out = _moe(xn2, comb, x1, lw).reshape(b, seq_len, D_MODEL)
    up3 = up.reshape(b, seq_len, POOL_WIDTH)
    if seq_len >= POOL_HIST:
        pool_state = up3[:, seq_len - POOL_HIST:]
    else:
        pool_state = jnp.concatenate([hist16[:, 1:], up3], axis=1)[:, -POOL_HIST:]
    state = (lat.reshape(b, seq_len, MLA_KV_LORA),
             kr[:, MLA_NOPE:MLA_QK].reshape(b, seq_len, MLA_ROPE),
             ksb.reshape(b, seq_len, SB_HEADS, SB_HEAD_DIM),
             vsb.reshape(b, seq_len, SB_HEADS, SB_HEAD_DIM),
             pool_state)
    return x_out, state


def kernel(x_prompt, x_sample, cache_mla_latent, cache_mla_krope, cache_sb_k, cache_sb_v, state_pool,
           g_mix_norm, w_in, w_pool_lin, pool_scale, g_q_lora, w_uq, g_kv_lora, w_ukv, g_qk_q, g_qk_k,
           w_br_pool, w_br_mla, w_br_sb, w_out, g_ffn_norm, w_router_group, b_router_group,
           w_router_expert, b_router_expert, w_exp_gate, w_exp_up, w_exp_down):
    depth = w_in.shape[0]
    past = cache_mla_latent.shape[2]
    cs_p = _rope_table(jnp.arange(x_prompt.shape[1], dtype=jnp.int32))
    cs_s = _rope_table(past + jnp.arange(x_sample.shape[1], dtype=jnp.int32))
    xp, xs = x_prompt, x_sample
    new_p = [[] for _ in range(5)]
    new_s = [[] for _ in range(5)]
    for l in range(depth):
        lw = _prep_layer({
            "g_mix_norm": g_mix_norm[l], "w_in": w_in[l], "w_pool_lin": w_pool_lin[l],
            "pool_scale": pool_scale[l], "g_q_lora": g_q_lora[l], "w_uq": w_uq[l],
            "g_kv_lora": g_kv_lora[l], "w_ukv": w_ukv[l], "g_qk_q": g_qk_q[l], "g_qk_k": g_qk_k[l],
            "w_br_pool": w_br_pool[l], "w_br_mla": w_br_mla[l], "w_br_sb": w_br_sb[l],
            "w_out": w_out[l], "g_ffn_norm": g_ffn_norm[l],
            "w_router_group": w_router_group[l], "b_router_group": b_router_group[l],
            "w_router_expert": w_router_expert[l], "b_router_expert": b_router_expert[l],
            "w_exp_gate": w_exp_gate[l], "w_exp_up": w_exp_up[l], "w_exp_down": w_exp_down[l],
        })
        hist = {"lat": cache_mla_latent[l], "kr": cache_mla_krope[l], "sbk": cache_sb_k[l],
                "sbv": cache_sb_v[l], "pool": state_pool[l]}
        xp, st_p = _layer(xp, cs_p, lw, None, 0)
        xs, st_s = _layer(xs, cs_s, lw, hist, past)
        for i in range(5):
            new_p[i].append(st_p[i])
            new_s[i].append(st_s[i])
    return (xp, xs, *[jnp.stack(s) for s in new_p], *[jnp.stack(s) for s in new_s])
```

```python
import functools

import jax
import jax.numpy as jnp
from jax import lax
from jax.experimental import pallas as pl
from jax.experimental.pallas import tpu as pltpu

F32 = jnp.float32
BF16 = jnp.bfloat16

D_MODEL = 1024
CHUNK_SHIFT = 6
EPS = 1e-6
POOL_WIDTH = 256
POOL_HIST = 15
POOL_HALO = 16
POOL_TOP = 32
MLA_HEADS = 8
MLA_Q_LORA = 256
MLA_KV_LORA = 128
MLA_NOPE = 64
MLA_ROPE = 32
MLA_V = 64
MLA_QK = MLA_NOPE + MLA_ROPE
ROPE_THETA = 10000.0
SB_HEADS = 4
SB_HEAD_DIM = 64
SB_WIDTH = 256
N_GROUPS = 4
EXPERTS_PER_GROUP = 4
N_EXPERTS = 16
D_EXPERT = 256
LANES = 128
ROUTER_EXPERT_LANE0 = N_GROUPS
VMEM_LIMIT = 56 * 1024 * 1024
NEG_BIG = -1e30
LOG2_E = 1.4426950408889634
SB_SUB = 256
ATTN_TILE_ELEMS = 512 * 512
MLA_STEP_HEADS = 8
MLA_BOUND_SLACK = 1.02
MLA_SAFE_BOUND = 60.0

C_UPOOL = 0
C_CQ = 256
C_CKV = 512
C_QSB = 640
C_KSB = 896
C_VSB = 1152
C_KRA = 1408
C_KRB = 1536
C_TOTAL = 1664


def _tile(n, pref, mult=16):
    best = None
    for t in range(mult, min(n, pref) + 1, mult):
        if n % t == 0:
            best = t
    return best if best is not None else n


def _params(*sem, flags=None):
    return pltpu.CompilerParams(dimension_semantics=sem, vmem_limit_bytes=VMEM_LIMIT, flags=flags)


def _rms(x, inv_n=None):
    if inv_n is None:
        ms = jnp.mean(x * x, axis=-1, keepdims=True)
    else:
        ms = jnp.sum(x * x, axis=-1, keepdims=True) * inv_n
    return x * lax.rsqrt(ms + EPS)


def _dot(a, b):
    return jnp.dot(a, b, preferred_element_type=F32)


def _dot_nt(a, b):
    return lax.dot_general(a, b, (((1,), (1,)), ((), ())), preferred_element_type=F32)


def _inproj_kernel(x_ref, g_ref, w_ref, cs_ref, gq_ref, wq_ref, wqs_ref, gqk_ref, gkv_ref,
                   up_ref, q_ref, lat_ref, kr_ref, qsb_ref, ksb_ref, vsb_ref, ksbh_ref, vsbh_ref):
    x = x_ref[...]
    xn = (_rms(x) * g_ref[...]).astype(BF16)
    z = _dot(xn, w_ref[...])
    up_ref[...] = z[:, C_UPOOL:C_UPOOL + POOL_WIDTH]
    cq = z[:, C_CQ:C_CQ + MLA_Q_LORA]
    cqn = (_rms(cq) * gq_ref[...]).astype(BF16)
    qa = _dot(cqn, wq_ref[...])
    qb = _dot(cqn, wqs_ref[...])
    cos = cs_ref[0]
    sin = cs_ref[1]
    gqk = gqk_ref[...]
    for h in range(MLA_HEADS):
        sl = slice(h * LANES, (h + 1) * LANES)
        qh = qa[:, sl] * cos + qb[:, sl] * sin
        q_ref[h] = (_rms(qh, 1.0 / MLA_QK) * gqk).astype(BF16)
    ckv = z[:, C_CKV:C_CKV + MLA_KV_LORA]
    lat_ref[...] = _rms(ckv) * gkv_ref[...]
    kr_ref[...] = z[:, C_KRA:C_KRA + LANES] * cos + z[:, C_KRB:C_KRB + LANES] * sin
    qsb_ref[...] = (z[:, C_QSB:C_QSB + SB_WIDTH] * (SB_HEAD_DIM ** -0.5)).astype(BF16)
    ksb = z[:, C_KSB:C_KSB + SB_WIDTH]
    ksb_ref[...] = ksb
    ksbh_ref[...] = ksb.astype(BF16)
    vsb = z[:, C_VSB:C_VSB + SB_WIDTH]
    vsb_ref[...] = vsb
    vsbh_ref[...] = vsb.astype(BF16)


def _inproj(x2, cs, lw, seq_len):
    t = x2.shape[0]
    tm = _tile(t, 1024)
    if tm <= seq_len and seq_len % tm == 0:
        per = seq_len // tm
        cs_map = lambda i: (0, i % per, 0)
    else:
        assert tm % seq_len == 0
        cs = jnp.tile(cs, (1, tm // seq_len, 1))
        cs_map = lambda i: (0, 0, 0)
    row = lambda i: (i, 0)
    const2 = lambda i: (0, 0)
    out_shape = (
        jax.ShapeDtypeStruct((t, POOL_WIDTH), F32),
        jax.ShapeDtypeStruct((MLA_HEADS, t, LANES), BF16),
        jax.ShapeDtypeStruct((t, MLA_KV_LORA), F32),
        jax.ShapeDtypeStruct((t, LANES), F32),
        jax.ShapeDtypeStruct((t, SB_WIDTH), BF16),
        jax.ShapeDtypeStruct((t, SB_WIDTH), F32),
        jax.ShapeDtypeStruct((t, SB_WIDTH), F32),
        jax.ShapeDtypeStruct((t, SB_WIDTH), BF16),
        jax.ShapeDtypeStruct((t, SB_WIDTH), BF16),
    )
    out_specs = (
        pl.BlockSpec((tm, POOL_WIDTH), row),
        pl.BlockSpec((MLA_HEADS, tm, LANES), lambda i: (0, i, 0)),
        pl.BlockSpec((tm, MLA_KV_LORA), row),
        pl.BlockSpec((tm, LANES), row),
        pl.BlockSpec((tm, SB_WIDTH), row),
        pl.BlockSpec((tm, SB_WIDTH), row),
        pl.BlockSpec((tm, SB_WIDTH), row),
        pl.BlockSpec((tm, SB_WIDTH), row),
        pl.BlockSpec((tm, SB_WIDTH), row),
    )
    in_specs = [
        pl.BlockSpec((tm, D_MODEL), row),
        pl.BlockSpec((1, D_MODEL), const2),
        pl.BlockSpec((D_MODEL, C_TOTAL), const2),
        pl.BlockSpec((2, tm, LANES), cs_map),
        pl.BlockSpec((1, MLA_Q_LORA), const2),
        pl.BlockSpec((MLA_Q_LORA, MLA_HEADS * LANES), const2),
        pl.BlockSpec((MLA_Q_LORA, MLA_HEADS * LANES), const2),
        pl.BlockSpec((1, LANES), const2),
        pl.BlockSpec((1, MLA_KV_LORA), const2),
    ]
    return pl.pallas_call(
        _inproj_kernel, grid=(t // tm,), in_specs=in_specs, out_specs=out_specs, out_shape=out_shape,
        compiler_params=_params("parallel"), name="inproj",
    )(x2, lw["g_mix"], lw["w_cat"], cs, lw["g_q_lora"], lw["wq"], lw["wqs"], lw["g_qk_q"], lw["g_kv"])


def _mla_keys_kernel(lat_ref, kr_ref, wk_ref, wv_ref, g_ref, k_ref, v_ref):
    latb = lat_ref[...].astype(BF16)
    kn = _dot(latb, wk_ref[...])
    vv = _dot(latb, wv_ref[...])
    kr = kr_ref[...]
    g = g_ref[...]
    lane = lax.broadcasted_iota(jnp.int32, (1, LANES), 1)
    for h in range(MLA_HEADS):
        sl = slice(h * LANES, (h + 1) * LANES)
        kh = kn[:, sl] + kr
        k_ref[h] = (_rms(kh, 1.0 / MLA_QK) * g).astype(BF16)
        ones = (lane == _mla_ones_lane(h)).astype(F32)
        v_ref[h] = (vv[:, sl] + ones).astype(BF16)


def _mla_keys(lat2, kr2, lw):
    t = lat2.shape[0]
    tm = _tile(t, 1024)
    row = lambda i: (i, 0)
    const2 = lambda i: (0, 0)
    hd = lambda i: (0, i, 0)
    shp = jax.ShapeDtypeStruct((MLA_HEADS, t, LANES), BF16)
    return pl.pallas_call(
        _mla_keys_kernel, grid=(t // tm,),
        in_specs=[pl.BlockSpec((tm, MLA_KV_LORA), row), pl.BlockSpec((tm, LANES), row),
                  pl.BlockSpec((MLA_KV_LORA, MLA_HEADS * LANES), const2),
                  pl.BlockSpec((MLA_KV_LORA, MLA_HEADS * LANES), const2),
                  pl.BlockSpec((1, LANES), const2)],
        out_specs=(pl.BlockSpec((MLA_HEADS, tm, LANES), hd), pl.BlockSpec((MLA_HEADS, tm, LANES), hd)),
        out_shape=(shp, shp), compiler_params=_params("parallel"), name="mla_keys",
    )(lat2, kr2, lw["wk"], lw["wv"], lw["g_qk_k"])


def _aligned_diagonal(tq, tk, q_off, lq, kv_len):
    return tq == tk and q_off % tq == 0 and tq % (2 * LANES) == 0 and kv_len >= q_off + lq


def _mla_ones_lane(h):
    return MLA_V if h % 2 == 0 else 0


def _mla_attn_kernel(bound_ref, q_ref, k_ref, v_ref, o_ref, m_ref, acc_ref, *, tq, tk, q_off, kv_len,
                     quartered):
    i = pl.program_id(2)
    q0 = q_off + i * tq
    vis_all = jnp.minimum(((q0 >> CHUNK_SHIFT) + 1) << CHUNK_SHIFT, kv_len)
    vis_any = jnp.minimum((((q0 + tq - 1) >> CHUNK_SHIFT) + 1) << CHUNK_SHIFT, kv_len)
    n_full = vis_all // tk
    n_blk = (vis_any + tk - 1) // tk
    c = (MLA_QK ** -0.5) * LOG2_E
    bound = bound_ref[0]
    acc_ref[...] = jnp.zeros(acc_ref.shape, F32)

    def block(rows, start, width, ok, bounded):
        for hh in range(MLA_STEP_HEADS):
            s = _dot_nt(q_ref[hh, rows], k_ref[hh, pl.ds(start, width), :])
            if ok is not None:
                s = jnp.where(ok, s, -jnp.inf)
            v = v_ref[hh, pl.ds(start, width), :]
            if bounded:
                acc_ref[hh, rows] += _dot(jnp.exp2(s * c - bound).astype(BF16), v)
            else:
                m_prev = m_ref[hh, rows]
                m_new = jnp.maximum(m_prev, jnp.max(s, axis=1, keepdims=True))
                alpha = jnp.exp2((m_prev - m_new) * c)
                p = jnp.exp2((s - jnp.tile(m_new, (1, width // LANES))) * c)
                acc_ref[hh, rows] = alpha * acc_ref[hh, rows] + _dot(p.astype(BF16), v)
                m_ref[hh, rows] = m_new

    def step(j, masked, bounded):
        start = pl.multiple_of(j * tk, tk)
        ok = None
        if masked:
            qpos = q0 + lax.broadcasted_iota(jnp.int32, (tq, tk), 0)
            kpos = start + lax.broadcasted_iota(jnp.int32, (tq, tk), 1)
            ok = ((kpos >> CHUNK_SHIFT) <= (qpos >> CHUNK_SHIFT)) & (kpos < kv_len)
        block(slice(0, tq), start, tk, ok, bounded)

    def diagonal(bounded):
        half = tq // 2
        ok = ((lax.broadcasted_iota(jnp.int32, (half, half), 1) >> CHUNK_SHIFT) <=
              (lax.broadcasted_iota(jnp.int32, (half, half), 0) >> CHUNK_SHIFT))
        start = pl.multiple_of(q0, tq)
        block(slice(0, half), start, half, ok, bounded)
        block(slice(half, tq), start, half, None, bounded)
        block(slice(half, tq), pl.multiple_of(q0 + half, half), half, ok, bounded)

    def loops(bounded):
        lax.fori_loop(0, n_full, lambda j, carry: (step(j, False, bounded), carry)[1], 0)
        if quartered:
            diagonal(bounded)
        else:
            lax.fori_loop(n_full, n_blk, lambda j, carry: (step(j, True, bounded), carry)[1], 0)

    @pl.when(bound <= MLA_SAFE_BOUND)
    def _():
        loops(True)

    @pl.when(bound > MLA_SAFE_BOUND)
    def _():
        m_ref[...] = jnp.full(m_ref.shape, NEG_BIG, F32)
        loops(False)

    lane = lax.broadcasted_iota(jnp.int32, (1, LANES), 1)
    for pair in range(MLA_STEP_HEADS // 2):
        outs = []
        for hh in (2 * pair, 2 * pair + 1):
            acc = acc_ref[hh]
            denom = jnp.sum(jnp.where(lane == _mla_ones_lane(hh), acc, 0.0), axis=1, keepdims=True)
            outs.append(acc / denom)
        o_ref[:, pair * LANES:(pair + 1) * LANES] = jnp.where(lane < MLA_V, outs[0], outs[1]).astype(BF16)


def _mla_attn(bound, q, k, v, batch, lq, lk, q_off, kv_len):
    tq = _tile(lq, 512)
    tk = _tile(lk, ATTN_TILE_ELEMS // tq, LANES)
    assert tk % LANES == 0
    nq = lq // tq
    nh = MLA_STEP_HEADS
    kern = functools.partial(_mla_attn_kernel, tq=tq, tk=tk, q_off=q_off, kv_len=kv_len,
                             quartered=_aligned_diagonal(tq, tk, q_off, lq, kv_len))
    return pl.pallas_call(
        kern, grid=(batch, MLA_HEADS // nh, nq),
        in_specs=[pl.BlockSpec(memory_space=pltpu.SMEM),
                  pl.BlockSpec((nh, tq, LANES), lambda b, p, i: (p, b * nq + i, 0)),
                  pl.BlockSpec((nh, lk, LANES), lambda b, p, i: (p, b, 0)),
                  pl.BlockSpec((nh, lk, LANES), lambda b, p, i: (p, b, 0))],
        out_specs=pl.BlockSpec((tq, nh * MLA_V), lambda b, p, i: (b * nq + i, p)),
        out_shape=jax.ShapeDtypeStruct((batch * lq, MLA_HEADS * MLA_V), BF16),
        scratch_shapes=[pltpu.VMEM((nh, tq, LANES), F32), pltpu.VMEM((nh, tq, LANES), F32)],
        compiler_params=_params("parallel", "parallel", "arbitrary"), name="mla_attn",
    )(bound, q, k, v)


def _sb_attn_kernel(q_ref, k_ref, v_ref, o_ref, c_ref, acc_ref, *, tq, tk, q_off, quartered):
    i = pl.program_id(2)
    q0 = q_off + i * tq
    n_full = q0 // tk
    n_blk = (q0 + tq - 2) // tk + 1
    lane = lax.broadcasted_iota(jnp.int32, (1, LANES), 1)
    upper = (lax.broadcasted_iota(jnp.int32, (SB_SUB, SB_SUB), 0) >
             lax.broadcasted_iota(jnp.int32, (SB_SUB, SB_SUB), 1)).astype(BF16)
    acc_ref[...] = jnp.zeros(acc_ref.shape, F32)
    c_ref[...] = jnp.zeros(c_ref.shape, F32)
    zero = jnp.zeros((), BF16)
    pair_lanes = [slice((hh // 2) * LANES, (hh // 2 + 1) * LANES) for hh in range(SB_HEADS)]
    in_head = [(lane >= (hh % 2) * SB_HEAD_DIM) & (lane < (hh % 2 + 1) * SB_HEAD_DIM) for hh in range(SB_HEADS)]
    qs = [jnp.where(in_head[hh], q_ref[:, pair_lanes[hh]], zero) for hh in range(SB_HEADS)]

    def block(rows, start, width, ok):
        for hh in range(SB_HEADS):
            k = k_ref[pl.ds(start, width), pair_lanes[hh]]
            v = v_ref[pl.ds(start, width), pair_lanes[hh]]
            z = _dot_nt(qs[hh][rows], k)
            t = jnp.log(1.0 + jnp.exp2(jnp.abs(z) * (-LOG2_E)))
            log_beta = jnp.minimum(z, 0.0) - t
            log_keep = log_beta - z
            if ok is not None:
                log_keep = jnp.where(ok, log_keep, 0.0)
            keep16 = log_keep.astype(BF16)
            parts = []
            later = None
            for sb in reversed(range(width // SB_SUB)):
                sl = slice(sb * SB_SUB, (sb + 1) * SB_SUB)
                inside = _dot(keep16[:, sl], upper)
                if later is not None:
                    inside = inside + later
                parts.append(jnp.exp(log_beta[:, sl] + inside))
                total = jnp.sum(log_keep[:, sl], axis=1, keepdims=True)
                later = total if later is None else later + total
            a = parts[0] if len(parts) == 1 else jnp.concatenate(parts[::-1], axis=1)
            if ok is not None:
                a = jnp.where(ok, a, 0.0)
            before = c_ref[hh, rows]
            pv = _dot(a.astype(BF16), jnp.where(in_head[hh], v, zero))
            acc_ref[rows, pair_lanes[hh]] += jnp.exp(before) * pv
            c_ref[hh, rows] = before + later

    def step(j, masked):
        start = pl.multiple_of(j * tk, tk)
        ok = None
        if masked:
            qpos = q0 + lax.broadcasted_iota(jnp.int32, (tq, tk), 0)
            kpos = start + lax.broadcasted_iota(jnp.int32, (tq, tk), 1)
            ok = kpos < qpos
        block(slice(0, tq), start, tk, ok)

    def masked_body(it, carry):
        step(n_blk - 1 - it, True)
        return carry

    def full_body(it, carry):
        step(n_full - 1 - it, False)
        return carry

    if quartered:
        half = tq // 2
        ok = (lax.broadcasted_iota(jnp.int32, (half, half), 1) <
              lax.broadcasted_iota(jnp.int32, (half, half), 0))
        start = pl.multiple_of(q0, tq)
        block(slice(half, tq), pl.multiple_of(q0 + half, half), half, ok)
        block(slice(half, tq), start, half, None)
        block(slice(0, half), start, half, ok)
    else:
        lax.fori_loop(0, n_blk - n_full, masked_body, 0)
    lax.fori_loop(0, n_full, full_body, 0)
    o_ref[...] = acc_ref[...].astype(BF16)


def _sb_attn(q, k, v, batch, lq, lk, q_off):
    tq = _tile(lq, 512)
    tk = _tile(lk, ATTN_TILE_ELEMS // tq, SB_SUB)
    assert tk % SB_SUB == 0
    nq = lq // tq
    quartered = _aligned_diagonal(tq, tk, q_off, lq, lk) and (tq // 2) % SB_SUB == 0
    kern = functools.partial(_sb_attn_kernel, tq=tq, tk=tk, q_off=q_off, quartered=quartered)
    return pl.pallas_call(
        kern, grid=(batch, 1, nq),
        in_specs=[pl.BlockSpec((tq, SB_WIDTH), lambda b, p, i: (b * nq + i, 0)),
                  pl.BlockSpec((lk, SB_WIDTH), lambda b, p, i: (b, 0)),
                  pl.BlockSpec((lk, SB_WIDTH), lambda b, p, i: (b, 0))],
        out_specs=pl.BlockSpec((tq, SB_WIDTH), lambda b, p, i: (b * nq + i, 0)),
        out_shape=jax.ShapeDtypeStruct((batch * lq, SB_WIDTH), BF16),
        scratch_shapes=[pltpu.VMEM((SB_HEADS, tq, LANES), F32), pltpu.VMEM((tq, SB_WIDTH), F32)],
        compiler_params=_params("parallel", "parallel", "arbitrary"), name="sb_attn",
    )(q, k, v)


def _pool_kernel(u_ref, halo_ref, hist_ref, w_ref, sc_ref, y_ref, s0, s1, s2, s3, *, tm, past):
    i = pl.program_id(1)
    top = POOL_TOP
    end = top + tm
    s0[0:top - POOL_HALO, :] = jnp.zeros((top - POOL_HALO, POOL_WIDTH), F32)
    s0[top:end, :] = u_ref[...]

    @pl.when(i == 0)
    def _():
        s0[top - POOL_HALO:top, :] = hist_ref[0]

    @pl.when(i > 0)
    def _():
        s0[top - POOL_HALO:top, :] = halo_ref[...]

    s1[8:end, :] = s0[8:end, :] + s0[7:end - 1, :]
    s2[16:end, :] = s1[16:end, :] + s1[14:end - 2, :]
    s3[24:end, :] = s2[24:end, :] + s2[20:end - 4, :]
    x0 = s0[top:end, :]
    a2 = s1[top:end, :]
    a4 = s2[top:end, :]
    a8 = s3[top:end, :]
    a16 = a8 + s3[top - 8:end - 8, :]
    grp = lax.broadcasted_iota(jnp.int32, (1, POOL_WIDTH), 1) >> 6
    win = jnp.where(grp == 0, a2, jnp.where(grp == 1, a4, jnp.where(grp == 2, a8, a16)))
    width = jnp.where(grp == 0, 2.0, jnp.where(grp == 1, 4.0, jnp.where(grp == 2, 8.0, 16.0)))
    pos = past + i * tm + lax.broadcasted_iota(jnp.int32, (tm, 1), 0)
    cnt = jnp.minimum((pos + 1).astype(F32), width)
    d = win / cnt - x0
    y_ref[...] = (_dot(d.astype(BF16), w_ref[...]) * sc_ref[...]).astype(BF16)


def _pool(u2, hist16, lw, batch, seq_len, past):
    tm = _tile(seq_len, 512)
    nl = seq_len // tm
    per = tm // POOL_HALO
    kern = functools.partial(_pool_kernel, tm=tm, past=past)
    return pl.pallas_call(
        kern, grid=(batch, nl),
        in_specs=[pl.BlockSpec((tm, POOL_WIDTH), lambda b, i: (b * nl + i, 0)),
                  pl.BlockSpec((POOL_HALO, POOL_WIDTH),
                               lambda b, i: (jnp.maximum((b * nl + i) * per - 1, 0), 0)),
                  pl.BlockSpec((1, POOL_HALO, POOL_WIDTH), lambda b, i: (b, 0, 0)),
                  pl.BlockSpec((POOL_WIDTH, POOL_WIDTH), lambda b, i: (0, 0)),
                  pl.BlockSpec((1, POOL_WIDTH), lambda b, i: (0, 0))],
        out_specs=pl.BlockSpec((tm, POOL_WIDTH), lambda b, i: (b * nl + i, 0)),
        out_shape=jax.ShapeDtypeStruct((batch * seq_len, POOL_WIDTH), BF16),
        scratch_shapes=[pltpu.VMEM((POOL_TOP + tm, POOL_WIDTH), F32)] * 4,
        compiler_params=_params("parallel", "arbitrary"), name="pool",
    )(u2, u2, hist16, lw["w_pool"], lw["pool_scale"])


def _sigmoid(x):
    return 1.0 / (1.0 + jnp.exp(-x))


def _merge_kernel(x_ref, yp_ref, ym_ref, ys_ref, g1_ref, wg_ref, wbp_ref, wbm_ref, wbs_ref, wo_ref,
                  g2_ref, wr_ref, br_ref, x1_ref, xn2_ref, comb_ref):
    x = x_ref[...]
    xn = (_rms(x) * g1_ref[...]).astype(BF16)
    h = _sigmoid(_dot(xn, wg_ref[:, 0:D_MODEL])) * _dot(yp_ref[...], wbp_ref[...])
    h = h + _sigmoid(_dot(xn, wg_ref[:, D_MODEL:2 * D_MODEL])) * _dot(ym_ref[...], wbm_ref[...])
    h = h + _sigmoid(_dot(xn, wg_ref[:, 2 * D_MODEL:3 * D_MODEL])) * _dot(ys_ref[...], wbs_ref[...])
    x1 = x + _dot(h.astype(BF16), wo_ref[...])
    x1_ref[...] = x1
    xn2 = (_rms(x1) * g2_ref[...]).astype(BF16)
    xn2_ref[...] = xn2
    lg = _dot(xn2, wr_ref[...]) + br_ref[...]
    lane = lax.broadcasted_iota(jnp.int32, lg.shape, 1)
    lanef = lane.astype(F32)
    neg = -jnp.inf
    big = 1e9
    is_g = lane < N_GROUPS
    glog = jnp.where(is_g, lg, neg)
    gmax = jnp.max(glog, axis=1, keepdims=True)
    grp = jnp.min(jnp.where(glog == gmax, lanef, big), axis=1, keepdims=True)
    p_grp = 1.0 / jnp.sum(jnp.where(is_g, jnp.exp(lg - gmax), 0.0), axis=1, keepdims=True)
    eid = lanef - float(ROUTER_EXPERT_LANE0)
    lo = grp * float(EXPERTS_PER_GROUP)
    in_g = (eid >= lo) & (eid < lo + float(EXPERTS_PER_GROUP))
    e1 = jnp.where(in_g, lg, neg)
    v1 = jnp.max(e1, axis=1, keepdims=True)
    i1 = jnp.min(jnp.where(in_g & (e1 == v1), eid, big), axis=1, keepdims=True)
    rest = in_g & (eid != i1)
    e2 = jnp.where(rest, lg, neg)
    v2 = jnp.max(e2, axis=1, keepdims=True)
    i2 = jnp.min(jnp.where(rest & (e2 == v2), eid, big), axis=1, keepdims=True)
    ex = jnp.exp(v2 - v1)
    w1 = (1.0 / (1.0 + ex)) * p_grp
    w2 = (ex / (1.0 + ex)) * p_grp
    comb_ref[...] = jnp.where(eid == i1, w1, 0.0) + jnp.where(eid == i2, w2, 0.0)


def _merge(x2, yp, ym, ys, lw):
    t = x2.shape[0]
    tm = _tile(t, 1024)
    row = lambda i: (i, 0)
    const2 = lambda i: (0, 0)
    resident = pl.Buffered(1)
    return pl.pallas_call(
        _merge_kernel, grid=(t // tm,),
        in_specs=[pl.BlockSpec((tm, D_MODEL), row),
                  pl.BlockSpec((tm, POOL_WIDTH), row),
                  pl.BlockSpec((tm, MLA_HEADS * MLA_V), row),
                  pl.BlockSpec((tm, SB_WIDTH), row),
                  pl.BlockSpec((1, D_MODEL), const2),
                  pl.BlockSpec((D_MODEL, 3 * D_MODEL), const2, pipeline_mode=resident),
                  pl.BlockSpec((POOL_WIDTH, D_MODEL), const2, pipeline_mode=resident),
                  pl.BlockSpec((MLA_HEADS * MLA_V, D_MODEL), const2, pipeline_mode=resident),
                  pl.BlockSpec((SB_WIDTH, D_MODEL), const2, pipeline_mode=resident),
                  pl.BlockSpec((D_MODEL, D_MODEL), const2, pipeline_mode=resident),
                  pl.BlockSpec((1, D_MODEL), const2),
                  pl.BlockSpec((D_MODEL, LANES), const2),
                  pl.BlockSpec((1, LANES), const2)],
        out_specs=(pl.BlockSpec((tm, D_MODEL), row), pl.BlockSpec((tm, D_MODEL), row),
                   pl.BlockSpec((tm, LANES), row)),
        out_shape=(jax.ShapeDtypeStruct((t, D_MODEL), F32), jax.ShapeDtypeStruct((t, D_MODEL), BF16),
                   jax.ShapeDtypeStruct((t, LANES), F32)),
        compiler_params=_params("parallel"), name="merge",
    )(x2, yp, ym, ys, lw["g_mix"], lw["w_gate"], lw["w_br_pool"], lw["w_br_mla"], lw["w_br_sb"],
      lw["w_out"], lw["g_ffn"], lw["w_router"], lw["b_router"])


def _moe_kernel(xn_ref, comb_ref, x1_ref, wg_ref, wu_ref, wd_ref, o_ref):
    xn = xn_ref[...]
    comb = comb_ref[...]
    group_w = EXPERTS_PER_GROUP * D_EXPERT
    y = x1_ref[...]
    for g in range(N_GROUPS):
        cols = slice(g * group_w, (g + 1) * group_w)
        a = _dot(xn, wg_ref[:, cols])
        h = (a * _sigmoid(a)) * _dot(xn, wu_ref[:, cols])
        scaled = []
        for j in range(EXPERTS_PER_GROUP):
            lane = ROUTER_EXPERT_LANE0 + g * EXPERTS_PER_GROUP + j
            scaled.append((h[:, j * D_EXPERT:(j + 1) * D_EXPERT] * comb[:, lane:lane + 1]).astype(BF16))
        y = y + _dot(jnp.concatenate(scaled, axis=1), wd_ref[cols, :])
    o_ref[...] = y


def _moe(xn2, comb, x1, lw):
    t = xn2.shape[0]
    tm = _tile(t, 512)
    row = lambda i: (i, 0)
    const2 = lambda i: (0, 0)
    hidden = N_EXPERTS * D_EXPERT
    resident = pl.Buffered(1)
    return pl.pallas_call(
        _moe_kernel, grid=(t // tm,),
        in_specs=[pl.BlockSpec((tm, D_MODEL), row), pl.BlockSpec((tm, LANES), row),
                  pl.BlockSpec((tm, D_MODEL), row),
                  pl.BlockSpec((D_MODEL, hidden), const2, pipeline_mode=resident),
                  pl.BlockSpec((D_MODEL, hidden), const2, pipeline_mode=resident),
                  pl.BlockSpec((hidden, D_MODEL), const2, pipeline_mode=resident)],
        out_specs=pl.BlockSpec((tm, D_MODEL), row),
        out_shape=jax.ShapeDtypeStruct((t, D_MODEL), F32),
        compiler_params=_params("parallel"), name="moe",
    )(xn2, comb, x1, lw["w_exp_gate"], lw["w_exp_up"], lw["w_exp_down"])


def _head_blocks(w, n_heads, width, place):
    k = w.shape[0]
    blocks = [place(h, w[:, h * width:(h + 1) * width]) for h in range(n_heads)]
    return jnp.concatenate(blocks, axis=1).reshape(k, n_heads * LANES)


def _pad_lanes(w, before, total=LANES):
    return jnp.pad(w, ((0, 0), (before, total - before - w.shape[1])))


def _swap_halves(w):
    half = w.shape[1] // 2
    return jnp.concatenate([-w[:, half:], w[:, :half]], axis=1)


def _experts_side_by_side(w):
    e, d, f = w.shape
    return jnp.transpose(w.astype(BF16), (1, 0, 2)).reshape(d, e * f)


def _prep_layer(p):
    w_in = p["w_in"].astype(BF16)
    w_uq = p["w_uq"].astype(BF16)
    w_ukv = p["w_ukv"].astype(BF16)
    w_kr = w_in[:, 640:672]
    w_cat = jnp.concatenate(
        [w_in[:, 0:640], w_in[:, 672:1440],
         _pad_lanes(w_kr, MLA_NOPE), _pad_lanes(_swap_halves(w_kr), MLA_NOPE)], axis=1)
    wq = _head_blocks(w_uq, MLA_HEADS, MLA_QK, lambda h, b: _pad_lanes(b, 0))
    wqs = _head_blocks(w_uq, MLA_HEADS, MLA_QK,
                       lambda h, b: _pad_lanes(_swap_halves(b[:, MLA_NOPE:]), MLA_NOPE))
    wk = _head_blocks(w_ukv, MLA_HEADS, MLA_NOPE + MLA_V,
                      lambda h, b: _pad_lanes(b[:, :MLA_NOPE], 0))
    wv = _head_blocks(w_ukv, MLA_HEADS, MLA_NOPE + MLA_V,
                      lambda h, b: _pad_lanes(b[:, MLA_NOPE:], (h % 2) * MLA_V))
    w_pool = jax.scipy.linalg.block_diag(*[p["w_pool_lin"][g] for g in range(4)])
    w_router = _pad_lanes(jnp.concatenate([p["w_router_group"], p["w_router_expert"]], axis=1), 0)
    b_router = _pad_lanes(jnp.concatenate([p["b_router_group"], p["b_router_expert"]])[None, :], 0)
    return {
        "g_mix": p["g_mix_norm"][None, :],
        "w_cat": w_cat.astype(BF16),
        "g_q_lora": p["g_q_lora"][None, :],
        "wq": wq.astype(BF16), "wqs": wqs.astype(BF16),
        "g_qk_q": _pad_lanes(p["g_qk_q"][None, :], 0),
        "g_kv": p["g_kv_lora"][None, :],
        "wk": wk.astype(BF16), "wv": wv.astype(BF16),
        "g_qk_k": _pad_lanes(p["g_qk_k"][None, :], 0),
        "mla_bound": ((MLA_QK ** 0.5) * LOG2_E * MLA_BOUND_SLACK
                      * jnp.max(jnp.abs(p["g_qk_q"])) * jnp.max(jnp.abs(p["g_qk_k"]))).reshape(1),
        "w_pool": w_pool.astype(BF16),
        "pool_scale": p["pool_scale"][None, :],
        "w_gate": w_in[:, 1440:].astype(BF16),
        "w_br_pool": p["w_br_pool"].astype(BF16),
        "w_br_mla": p["w_br_mla"].astype(BF16),
        "w_br_sb": p["w_br_sb"].astype(BF16),
        "w_out": p["w_out"].astype(BF16),
        "g_ffn": p["g_ffn_norm"][None, :],
        "w_router": w_router.astype(BF16),
        "b_router": b_router,
        "w_exp_gate": _experts_side_by_side(p["w_exp_gate"]),
        "w_exp_up": _experts_side_by_side(p["w_exp_up"]),
        "w_exp_down": p["w_exp_down"].astype(BF16).reshape(N_EXPERTS * D_EXPERT, D_MODEL),
    }


def _rope_table(pos):
    half = MLA_ROPE // 2
    inv = ROPE_THETA ** (-jnp.arange(half, dtype=F32) / half)
    ang = pos.astype(F32)[:, None] * inv[None, :]
    cos, sin = jnp.cos(ang), jnp.sin(ang)
    n = pos.shape[0]
    cos_t = jnp.concatenate([jnp.ones((n, MLA_NOPE), F32), cos, cos, jnp.zeros((n, LANES - MLA_QK), F32)], axis=1)
    sin_t = jnp.concatenate([jnp.zeros((n, MLA_NOPE), F32), sin, sin, jnp.zeros((n, LANES - MLA_QK), F32)], axis=1)
    return jnp.stack([cos_t, sin_t])


def _rows_into(rows, old, new, lane0=0, width=None):
    b, n_old, w_old = old.shape
    width = w_old if width is None else width
    old = jnp.pad(old.astype(new.dtype), ((0, 0), (0, 0), (lane0, width - lane0 - w_old)))
    new = new.reshape(b, -1, width)
    tail = jnp.zeros((b, rows - n_old - new.shape[1], width), new.dtype)
    return jnp.concatenate([old, new, tail], axis=1)


def _layer(x, cs, lw, hist, past):
    b, seq_len, _ = x.shape
    t = b * seq_len
    x2 = x.reshape(t, D_MODEL)
    up, q, lat, kr, qsb, ksb, vsb, ksbh, vsbh = _inproj(x2, cs, lw, seq_len)
    if hist is None:
        lk = seq_len
        kv_len = seq_len
        lat_all, kr_all, ksb_all, vsb_all = lat, kr, ksbh, vsbh
        hist16 = jnp.zeros((b, POOL_HALO, POOL_WIDTH), F32)
    else:
        kv_len = past + seq_len
        lk = -(-kv_len // 256) * 256
        lat_all = _rows_into(lk, hist["lat"], lat).reshape(b * lk, MLA_KV_LORA)
        kr_all = _rows_into(lk, hist["kr"], kr, MLA_NOPE, LANES).reshape(b * lk, LANES)
        ksb_all = _rows_into(lk, hist["sbk"].reshape(b, past, SB_WIDTH), ksbh).reshape(b * lk, SB_WIDTH)
        vsb_all = _rows_into(lk, hist["sbv"].reshape(b, past, SB_WIDTH), vsbh).reshape(b * lk, SB_WIDTH)
        hist16 = jnp.pad(hist["pool"], ((0, 0), (POOL_HALO - POOL_HIST, 0), (0, 0)))
    k_mla, v_mla = _mla_keys(lat_all, kr_all, lw)
    y_mla = _mla_attn(lw["mla_bound"], q, k_mla, v_mla, b, seq_len, lk, past, kv_len)
    y_sb = _sb_attn(qsb, ksb_all, vsb_all, b, seq_len, lk, past)
    y_pool = _pool(up, hist16, lw, b, seq_len, past)
    x1, xn2, comb = _merge(x2, y_pool, y_mla, y_sb, lw)
    x_out = _moe(xn2, comb, x1, lw).reshape(b, seq_len, D_MODEL)
    up3 = up.reshape(b, seq_len, POOL_WIDTH)
    if seq_len >= POOL_HIST:
        pool_state = up3[:, seq_len - POOL_HIST:]
    else:
        pool_state = jnp.concatenate([hist16[:, 1:], up3], axis=1)[:, -POOL_HIST:]
    state = (lat.reshape(b, seq_len, MLA_KV_LORA),
             kr[:, MLA_NOPE:MLA_QK].reshape(b, seq_len, MLA_ROPE),
             ksb.reshape(b, seq_len, SB_HEADS, SB_HEAD_DIM),
             vsb.reshape(b, seq_len, SB_HEADS, SB_HEAD_DIM),
             pool_state)
    return x_out, state


def kernel(x_prompt, x_sample, cache_mla_latent, cache_mla_krope, cache_sb_k, cache_sb_v, state_pool,
           g_mix_norm, w_in, w_pool_lin, pool_scale, g_q_lora, w_uq, g_kv_lora, w_ukv, g_qk_q, g_qk_k,
           w_br_pool, w_br_mla, w_br_sb, w_out, g_ffn_norm, w_router_group, b_router_group,
           w_router_expert, b_router_expert, w_exp_gate, w_exp_up, w_exp_down):
    depth = w_in.shape[0]
    past = cache_mla_latent.shape[2]
    cs_p = _rope_table(jnp.arange(x_prompt.shape[1], dtype=jnp.int32))
    cs_s = _rope_table(past + jnp.arange(x_sample.shape[1], dtype=jnp.int32))
    xp, xs = x_prompt, x_sample
    new_p = [[] for _ in range(5)]
    new_s = [[] for _ in range(5)]
    for l in range(depth):
        lw = _prep_layer({
            "g_mix_norm": g_mix_norm[l], "w_in": w_in[l], "w_pool_lin": w_pool_lin[l],
            "pool_scale": pool_scale[l], "g_q_lora": g_q_lora[l], "w_uq": w_uq[l],
            "g_kv_lora": g_kv_lora[l], "w_ukv": w_ukv[l], "g_qk_q": g_qk_q[l], "g_qk_k": g_qk_k[l],
            "w_br_pool": w_br_pool[l], "w_br_mla": w_br_mla[l], "w_br_sb": w_br_sb[l],
            "w_out": w_out[l], "g_ffn_norm": g_ffn_norm[l],
            "w_router_group": w_router_group[l], "b_router_group": b_router_group[l],
            "w_router_expert": w_router_expert[l], "b_router_expert": b_router_expert[l],
            "w_exp_gate": w_exp_gate[l], "w_exp_up": w_exp_up[l], "w_exp_down": w_exp_down[l],
        })
        hist = {"lat": cache_mla_latent[l], "kr": cache_mla_krope[l], "sbk": cache_sb_k[l],
                "sbv": cache_sb_v[l], "pool": state_pool[l]}
        xp, st_p = _layer(xp, cs_p, lw, None, 0)
        xs, st_s = _layer(xs, cs_s, lw, hist, past)
        for i in range(5):
            new_p[i].append(st_p[i])
            new_s[i].append(st_s[i])
    return (xp, xs, *[jnp.stack(s) for s in new_p], *[jnp.stack(s) for s in new_s])
```

```python
import functools

import jax
import jax.numpy as jnp
from jax import lax
from jax.experimental import pallas as pl
from jax.experimental.pallas import tpu as pltpu

F32 = jnp.float32
BF16 = jnp.bfloat16

D_MODEL = 1024
CHUNK_SHIFT = 6
EPS = 1e-6
POOL_WIDTH = 256
POOL_HIST = 15
POOL_HALO = 16
POOL_TOP = 32
MLA_HEADS = 8
MLA_Q_LORA = 256
MLA_KV_LORA = 128
MLA_NOPE = 64
MLA_ROPE = 32
MLA_V = 64
MLA_QK = MLA_NOPE + MLA_ROPE
ROPE_THETA = 10000.0
SB_HEADS = 4
SB_HEAD_DIM = 64
SB_WIDTH = 256
N_GROUPS = 4
EXPERTS_PER_GROUP = 4
N_EXPERTS = 16
D_EXPERT = 256
LANES = 128
ROUTER_EXPERT_LANE0 = N_GROUPS
VMEM_LIMIT = 56 * 1024 * 1024
NEG_BIG = -1e30
LOG2_E = 1.4426950408889634
SB_SUB = 256
ATTN_TILE_ELEMS = 512 * 512
MLA_STEP_HEADS = 8
MLA_BOUND_SLACK = 1.02
MLA_SAFE_BOUND = 60.0

C_UPOOL = 0
C_CQ = 256
C_CKV = 512
C_QSB = 640
C_KSB = 896
C_VSB = 1152
C_KRA = 1408
C_KRB = 1536
C_TOTAL = 1664


def _tile(n, pref, mult=16):
    best = None
    for t in range(mult, min(n, pref) + 1, mult):
        if n % t == 0:
            best = t
    return best if best is not None else n


def _params(*sem, flags=None):
    return pltpu.CompilerParams(dimension_semantics=sem, vmem_limit_bytes=VMEM_LIMIT, flags=flags)


def _rms(x, inv_n=None):
    if inv_n is None:
        ms = jnp.mean(x * x, axis=-1, keepdims=True)
    else:
        ms = jnp.sum(x * x, axis=-1, keepdims=True) * inv_n
    return x * lax.rsqrt(ms + EPS)


def _dot(a, b):
    return jnp.dot(a, b, preferred_element_type=F32)


def _dot_nt(a, b):
    return lax.dot_general(a, b, (((1,), (1,)), ((), ())), preferred_element_type=F32)


def _inproj_kernel(x_ref, g_ref, w_ref, cs_ref, gq_ref, wq_ref, wqs_ref, gqk_ref, gkv_ref, wk_ref, wv_ref, gqkk_ref,
                   up_ref, q_ref, lat_ref, kr_ref, qsb_ref, ksb_ref, vsb_ref, ksbh_ref, vsbh_ref, krc_ref,
                   k_ref, v_ref):
    x = x_ref[...]
    xn = (_rms(x) * g_ref[...]).astype(BF16)
    z = _dot(xn, w_ref[...])
    up_ref[...] = z[:, C_UPOOL:C_UPOOL + POOL_WIDTH]
    cq = z[:, C_CQ:C_CQ + MLA_Q_LORA]
    cqn = (_rms(cq) * gq_ref[...]).astype(BF16)
    qa = _dot(cqn, wq_ref[...])
    qb = _dot(cqn, wqs_ref[...])
    cos = cs_ref[0]
    sin = cs_ref[1]
    gqk = gqk_ref[...]
    for h in range(MLA_HEADS):
        sl = slice(h * LANES, (h + 1) * LANES)
        qh = qa[:, sl] * cos + qb[:, sl] * sin
        q_ref[h] = (_rms(qh, 1.0 / MLA_QK) * gqk).astype(BF16)
    ckv = z[:, C_CKV:C_CKV + MLA_KV_LORA]
    lat = _rms(ckv) * gkv_ref[...]
    lat_ref[...] = lat
    kr = z[:, C_KRA:C_KRA + LANES] * cos + z[:, C_KRB:C_KRB + LANES] * sin
    kr_ref[...] = kr
    krc_ref[...] = kr[:, MLA_NOPE:MLA_QK]
    _store_mla_keys(lat, kr, wk_ref, wv_ref, gqkk_ref, k_ref, v_ref)
    qsb_ref[...] = (z[:, C_QSB:C_QSB + SB_WIDTH] * (SB_HEAD_DIM ** -0.5)).astype(BF16)
    ksb = z[:, C_KSB:C_KSB + SB_WIDTH]
    ksb_ref[...] = ksb
    ksbh_ref[...] = ksb.astype(BF16)
    vsb = z[:, C_VSB:C_VSB + SB_WIDTH]
    vsb_ref[...] = vsb
    vsbh_ref[...] = vsb.astype(BF16)


def _inproj(x2, cs, lw, seq_len):
    t = x2.shape[0]
    tm = _tile(t, 1024)
    if tm <= seq_len and seq_len % tm == 0:
        per = seq_len // tm
        cs_map = lambda i: (0, i % per, 0)
    else:
        assert tm % seq_len == 0
        cs = jnp.tile(cs, (1, tm // seq_len, 1))
        cs_map = lambda i: (0, 0, 0)
    row = lambda i: (i, 0)
    const2 = lambda i: (0, 0)
    out_shape = (
        jax.ShapeDtypeStruct((t, POOL_WIDTH), F32),
        jax.ShapeDtypeStruct((MLA_HEADS, t, LANES), BF16),
        jax.ShapeDtypeStruct((t, MLA_KV_LORA), F32),
        jax.ShapeDtypeStruct((t, LANES), F32),
        jax.ShapeDtypeStruct((t, SB_WIDTH), BF16),
        jax.ShapeDtypeStruct((t, SB_WIDTH), F32),
        jax.ShapeDtypeStruct((t, SB_WIDTH), F32),
        jax.ShapeDtypeStruct((t, SB_WIDTH), BF16),
        jax.ShapeDtypeStruct((t, SB_WIDTH), BF16),
        jax.ShapeDtypeStruct((t, MLA_ROPE), F32),
        jax.ShapeDtypeStruct((MLA_HEADS, t, LANES), BF16),
        jax.ShapeDtypeStruct((MLA_HEADS, t, LANES), BF16),
    )
    out_specs = (
        pl.BlockSpec((tm, POOL_WIDTH), row),
        pl.BlockSpec((MLA_HEADS, tm, LANES), lambda i: (0, i, 0)),
        pl.BlockSpec((tm, MLA_KV_LORA), row),
        pl.BlockSpec((tm, LANES), row),
        pl.BlockSpec((tm, SB_WIDTH), row),
        pl.BlockSpec((tm, SB_WIDTH), row),
        pl.BlockSpec((tm, SB_WIDTH), row),
        pl.BlockSpec((tm, SB_WIDTH), row),
        pl.BlockSpec((tm, SB_WIDTH), row),
        pl.BlockSpec((tm, MLA_ROPE), row),
        pl.BlockSpec((MLA_HEADS, tm, LANES), lambda i: (0, i, 0)),
        pl.BlockSpec((MLA_HEADS, tm, LANES), lambda i: (0, i, 0)),
    )
    in_specs = [
        pl.BlockSpec((tm, D_MODEL), row),
        pl.BlockSpec((1, D_MODEL), const2),
        pl.BlockSpec((D_MODEL, C_TOTAL), const2),
        pl.BlockSpec((2, tm, LANES), cs_map),
        pl.BlockSpec((1, MLA_Q_LORA), const2),
        pl.BlockSpec((MLA_Q_LORA, MLA_HEADS * LANES), const2),
        pl.BlockSpec((MLA_Q_LORA, MLA_HEADS * LANES), const2),
        pl.BlockSpec((1, LANES), const2),
        pl.BlockSpec((1, MLA_KV_LORA), const2),
        pl.BlockSpec((MLA_KV_LORA, MLA_HEADS * LANES), const2),
        pl.BlockSpec((MLA_KV_LORA, MLA_HEADS * LANES), const2),
        pl.BlockSpec((1, LANES), const2),
    ]
    return pl.pallas_call(
        _inproj_kernel, grid=(t // tm,), in_specs=in_specs, out_specs=out_specs, out_shape=out_shape,
        compiler_params=_params("parallel"), name="inproj",
    )(x2, lw["g_mix"], lw["w_cat"], cs, lw["g_q_lora"], lw["wq"], lw["wqs"], lw["g_qk_q"], lw["g_kv"],
      lw["wk"], lw["wv"], lw["g_qk_k"])


def _mla_keys_kernel(lat_ref, kr_ref, wk_ref, wv_ref, g_ref, k_ref, v_ref):
    _store_mla_keys(lat_ref[...], kr_ref[...], wk_ref, wv_ref, g_ref, k_ref, v_ref)


def _store_mla_keys(lat, kr, wk_ref, wv_ref, g_ref, k_ref, v_ref):
    latb = lat.astype(BF16)
    kn = _dot(latb, wk_ref[...])
    vv = _dot(latb, wv_ref[...])
    g = g_ref[...]
    lane = lax.broadcasted_iota(jnp.int32, (1, LANES), 1)
    for h in range(MLA_HEADS):
        sl = slice(h * LANES, (h + 1) * LANES)
        kh = kn[:, sl] + kr
        k_ref[h] = (_rms(kh, 1.0 / MLA_QK) * g).astype(BF16)
        ones = (lane == _mla_ones_lane(h)).astype(F32)
        v_ref[h] = (vv[:, sl] + ones).astype(BF16)


def _mla_keys(lat2, kr2, lw):
    t = lat2.shape[0]
    tm = _tile(t, 1024)
    row = lambda i: (i, 0)
    const2 = lambda i: (0, 0)
    hd = lambda i: (0, i, 0)
    shp = jax.ShapeDtypeStruct((MLA_HEADS, t, LANES), BF16)
    return pl.pallas_call(
        _mla_keys_kernel, grid=(t // tm,),
        in_specs=[pl.BlockSpec((tm, MLA_KV_LORA), row), pl.BlockSpec((tm, LANES), row),
                  pl.BlockSpec((MLA_KV_LORA, MLA_HEADS * LANES), const2),
                  pl.BlockSpec((MLA_KV_LORA, MLA_HEADS * LANES), const2),
                  pl.BlockSpec((1, LANES), const2)],
        out_specs=(pl.BlockSpec((MLA_HEADS, tm, LANES), hd), pl.BlockSpec((MLA_HEADS, tm, LANES), hd)),
        out_shape=(shp, shp), compiler_params=_params("parallel"), name="mla_keys",
    )(lat2, kr2, lw["wk"], lw["wv"], lw["g_qk_k"])


def _aligned_diagonal(tq, tk, q_off, lq, kv_len):
    return tq == tk and q_off % tq == 0 and tq % (2 * LANES) == 0 and kv_len >= q_off + lq


def _mla_ones_lane(h):
    return MLA_V if h % 2 == 0 else 0


def _mla_attn_kernel(bound_ref, q_ref, k_ref, v_ref, o_ref, m_ref, acc_ref, *, tq, tk, q_off, kv_len,
                     quartered):
    i = pl.program_id(2)
    q0 = q_off + i * tq
    vis_all = jnp.minimum(((q0 >> CHUNK_SHIFT) + 1) << CHUNK_SHIFT, kv_len)
    vis_any = jnp.minimum((((q0 + tq - 1) >> CHUNK_SHIFT) + 1) << CHUNK_SHIFT, kv_len)
    n_full = vis_all // tk
    n_blk = (vis_any + tk - 1) // tk
    c = (MLA_QK ** -0.5) * LOG2_E
    bound = bound_ref[0]
    acc_ref[...] = jnp.zeros(acc_ref.shape, F32)

    def block(rows, start, width, ok, bounded):
        for hh in range(MLA_STEP_HEADS):
            s = _dot_nt(q_ref[hh, rows], k_ref[hh, pl.ds(start, width), :])
            if ok is not None:
                s = jnp.where(ok, s, -jnp.inf)
            v = v_ref[hh, pl.ds(start, width), :]
            if bounded:
                acc_ref[hh, rows] += _dot(jnp.exp2(s * c - bound).astype(BF16), v)
            else:
                m_prev = m_ref[hh, rows]
                m_new = jnp.maximum(m_prev, jnp.max(s, axis=1, keepdims=True))
                alpha = jnp.exp2((m_prev - m_new) * c)
                p = jnp.exp2((s - jnp.tile(m_new, (1, width // LANES))) * c)
                acc_ref[hh, rows] = alpha * acc_ref[hh, rows] + _dot(p.astype(BF16), v)
                m_ref[hh, rows] = m_new

    def step(j, masked, bounded):
        start = pl.multiple_of(j * tk, tk)
        ok = None
        if masked:
            qpos = q0 + lax.broadcasted_iota(jnp.int32, (tq, tk), 0)
            kpos = start + lax.broadcasted_iota(jnp.int32, (tq, tk), 1)
            ok = ((kpos >> CHUNK_SHIFT) <= (qpos >> CHUNK_SHIFT)) & (kpos < kv_len)
        block(slice(0, tq), start, tk, ok, bounded)

    def diagonal(bounded):
        half = tq // 2
        ok = ((lax.broadcasted_iota(jnp.int32, (half, half), 1) >> CHUNK_SHIFT) <=
              (lax.broadcasted_iota(jnp.int32, (half, half), 0) >> CHUNK_SHIFT))
        start = pl.multiple_of(q0, tq)
        block(slice(0, half), start, half, ok, bounded)
        block(slice(half, tq), start, half, None, bounded)
        block(slice(half, tq), pl.multiple_of(q0 + half, half), half, ok, bounded)

    def loops(bounded):
        lax.fori_loop(0, n_full, lambda j, carry: (step(j, False, bounded), carry)[1], 0)
        if quartered:
            diagonal(bounded)
        else:
            lax.fori_loop(n_full, n_blk, lambda j, carry: (step(j, True, bounded), carry)[1], 0)

    @pl.when(bound <= MLA_SAFE_BOUND)
    def _():
        loops(True)

    @pl.when(bound > MLA_SAFE_BOUND)
    def _():
        m_ref[...] = jnp.full(m_ref.shape, NEG_BIG, F32)
        loops(False)

    lane = lax.broadcasted_iota(jnp.int32, (1, LANES), 1)
    for pair in range(MLA_STEP_HEADS // 2):
        outs = []
        for hh in (2 * pair, 2 * pair + 1):
            acc = acc_ref[hh]
            denom = jnp.sum(jnp.where(lane == _mla_ones_lane(hh), acc, 0.0), axis=1, keepdims=True)
            outs.append(acc / denom)
        o_ref[:, pair * LANES:(pair + 1) * LANES] = jnp.where(lane < MLA_V, outs[0], outs[1]).astype(BF16)


def _mla_attn(bound, q, k, v, batch, lq, lk, q_off, kv_len):
    tq = _tile(lq, 512)
    tk = _tile(lk, ATTN_TILE_ELEMS // tq, LANES)
    assert tk % LANES == 0
    nq = lq // tq
    nh = MLA_STEP_HEADS
    kern = functools.partial(_mla_attn_kernel, tq=tq, tk=tk, q_off=q_off, kv_len=kv_len,
                             quartered=_aligned_diagonal(tq, tk, q_off, lq, kv_len))
    return pl.pallas_call(
        kern, grid=(batch, MLA_HEADS // nh, nq),
        in_specs=[pl.BlockSpec(memory_space=pltpu.SMEM),
                  pl.BlockSpec((nh, tq, LANES), lambda b, p, i: (p, b * nq + i, 0)),
                  pl.BlockSpec((nh, lk, LANES), lambda b, p, i: (p, b, 0)),
                  pl.BlockSpec((nh, lk, LANES), lambda b, p, i: (p, b, 0))],
        out_specs=pl.BlockSpec((tq, nh * MLA_V), lambda b, p, i: (b * nq + i, p)),
        out_shape=jax.ShapeDtypeStruct((batch * lq, MLA_HEADS * MLA_V), BF16),
        scratch_shapes=[pltpu.VMEM((nh, tq, LANES), F32), pltpu.VMEM((nh, tq, LANES), F32)],
        compiler_params=_params("parallel", "parallel", "arbitrary"), name="mla_attn",
    )(bound, q, k, v)


def _sb_attn_kernel(q_ref, k_ref, v_ref, o_ref, c_ref, acc_ref, *, tq, tk, q_off, quartered):
    i = pl.program_id(2)
    q0 = q_off + i * tq
    n_full = q0 // tk
    n_blk = (q0 + tq - 2) // tk + 1
    lane = lax.broadcasted_iota(jnp.int32, (1, LANES), 1)
    upper = (lax.broadcasted_iota(jnp.int32, (SB_SUB, SB_SUB), 0) >
             lax.broadcasted_iota(jnp.int32, (SB_SUB, SB_SUB), 1)).astype(BF16)
    acc_ref[...] = jnp.zeros(acc_ref.shape, F32)
    c_ref[...] = jnp.zeros(c_ref.shape, F32)
    zero = jnp.zeros((), BF16)
    pair_lanes = [slice((hh // 2) * LANES, (hh // 2 + 1) * LANES) for hh in range(SB_HEADS)]
    in_head = [(lane >= (hh % 2) * SB_HEAD_DIM) & (lane < (hh % 2 + 1) * SB_HEAD_DIM) for hh in range(SB_HEADS)]
    qs = [jnp.where(in_head[hh], q_ref[:, pair_lanes[hh]], zero) for hh in range(SB_HEADS)]

    def block(rows, start, width, ok):
        for hh in range(SB_HEADS):
            k = k_ref[pl.ds(start, width), pair_lanes[hh]]
            v = v_ref[pl.ds(start, width), pair_lanes[hh]]
            z = _dot_nt(qs[hh][rows], k)
            t = jnp.log(1.0 + jnp.exp2(jnp.abs(z) * (-LOG2_E)))
            log_beta = jnp.minimum(z, 0.0) - t
            log_keep = log_beta - z
            if ok is not None:
                log_keep = jnp.where(ok, log_keep, 0.0)
            keep16 = log_keep.astype(BF16)
            parts = []
            later = None
            for sb in reversed(range(width // SB_SUB)):
                sl = slice(sb * SB_SUB, (sb + 1) * SB_SUB)
                inside = _dot(keep16[:, sl], upper)
                if later is not None:
                    inside = inside + later
                parts.append(jnp.exp(log_beta[:, sl] + inside))
                total = jnp.sum(log_keep[:, sl], axis=1, keepdims=True)
                later = total if later is None else later + total
            a = parts[0] if len(parts) == 1 else jnp.concatenate(parts[::-1], axis=1)
            if ok is not None:
                a = jnp.where(ok, a, 0.0)
            before = c_ref[hh, rows]
            pv = _dot(a.astype(BF16), jnp.where(in_head[hh], v, zero))
            acc_ref[rows, pair_lanes[hh]] += jnp.exp(before) * pv
            c_ref[hh, rows] = before + later

    def step(j, masked):
        start = pl.multiple_of(j * tk, tk)
        ok = None
        if masked:
            qpos = q0 + lax.broadcasted_iota(jnp.int32, (tq, tk), 0)
            kpos = start + lax.broadcasted_iota(jnp.int32, (tq, tk), 1)
            ok = kpos < qpos
        block(slice(0, tq), start, tk, ok)

    def masked_body(it, carry):
        step(n_blk - 1 - it, True)
        return carry

    def full_body(it, carry):
        step(n_full - 1 - it, False)
        return carry

    if quartered:
        half = tq // 2
        ok = (lax.broadcasted_iota(jnp.int32, (half, half), 1) <
              lax.broadcasted_iota(jnp.int32, (half, half), 0))
        start = pl.multiple_of(q0, tq)
        block(slice(half, tq), pl.multiple_of(q0 + half, half), half, ok)
        block(slice(half, tq), start, half, None)
        block(slice(0, half), start, half, ok)
    else:
        lax.fori_loop(0, n_blk - n_full, masked_body, 0)
    lax.fori_loop(0, n_full, full_body, 0)
    o_ref[...] = acc_ref[...].astype(BF16)


def _sb_attn(q, k, v, batch, lq, lk, q_off):
    tq = _tile(lq, 512)
    tk = _tile(lk, ATTN_TILE_ELEMS // tq, SB_SUB)
    assert tk % SB_SUB == 0
    nq = lq // tq
    quartered = _aligned_diagonal(tq, tk, q_off, lq, lk) and (tq // 2) % SB_SUB == 0
    kern = functools.partial(_sb_attn_kernel, tq=tq, tk=tk, q_off=q_off, quartered=quartered)
    return pl.pallas_call(
        kern, grid=(batch, 1, nq),
        in_specs=[pl.BlockSpec((tq, SB_WIDTH), lambda b, p, i: (b * nq + i, 0)),
                  pl.BlockSpec((lk, SB_WIDTH), lambda b, p, i: (b, 0)),
                  pl.BlockSpec((lk, SB_WIDTH), lambda b, p, i: (b, 0))],
        out_specs=pl.BlockSpec((tq, SB_WIDTH), lambda b, p, i: (b * nq + i, 0)),
        out_shape=jax.ShapeDtypeStruct((batch * lq, SB_WIDTH), BF16),
        scratch_shapes=[pltpu.VMEM((SB_HEADS, tq, LANES), F32), pltpu.VMEM((tq, SB_WIDTH), F32)],
        compiler_params=_params("parallel", "parallel", "arbitrary"), name="sb_attn",
    )(q, k, v)


def _pool_kernel(u_ref, halo_ref, hist_ref, w_ref, sc_ref, y_ref, s0, s1, s2, s3, *, tm, past):
    i = pl.program_id(1)
    top = POOL_TOP
    end = top + tm
    s0[0:top - POOL_HALO, :] = jnp.zeros((top - POOL_HALO, POOL_WIDTH), F32)
    s0[top:end, :] = u_ref[...]

    @pl.when(i == 0)
    def _():
        s0[top - POOL_HALO:top, :] = hist_ref[0]

    @pl.when(i > 0)
    def _():
        s0[top - POOL_HALO:top, :] = halo_ref[...]

    s1[8:end, :] = s0[8:end, :] + s0[7:end - 1, :]
    s2[16:end, :] = s1[16:end, :] + s1[14:end - 2, :]
    s3[24:end, :] = s2[24:end, :] + s2[20:end - 4, :]
    x0 = s0[top:end, :]
    a2 = s1[top:end, :]
    a4 = s2[top:end, :]
    a8 = s3[top:end, :]
    a16 = a8 + s3[top - 8:end - 8, :]
    grp = lax.broadcasted_iota(jnp.int32, (1, POOL_WIDTH), 1) >> 6
    win = jnp.where(grp == 0, a2, jnp.where(grp == 1, a4, jnp.where(grp == 2, a8, a16)))
    width = jnp.where(grp == 0, 2.0, jnp.where(grp == 1, 4.0, jnp.where(grp == 2, 8.0, 16.0)))
    pos = past + i * tm + lax.broadcasted_iota(jnp.int32, (tm, 1), 0)
    cnt = jnp.minimum((pos + 1).astype(F32), width)
    d = win / cnt - x0
    y_ref[...] = (_dot(d.astype(BF16), w_ref[...]) * sc_ref[...]).astype(BF16)


def _pool(u2, hist16, lw, batch, seq_len, past):
    tm = _tile(seq_len, 2048)
    nl = seq_len // tm
    per = tm // POOL_HALO
    kern = functools.partial(_pool_kernel, tm=tm, past=past)
    return pl.pallas_call(
        kern, grid=(batch, nl),
        in_specs=[pl.BlockSpec((tm, POOL_WIDTH), lambda b, i: (b * nl + i, 0)),
                  pl.BlockSpec((POOL_HALO, POOL_WIDTH),
                               lambda b, i: (jnp.maximum((b * nl + i) * per - 1, 0), 0)),
                  pl.BlockSpec((1, POOL_HALO, POOL_WIDTH), lambda b, i: (b, 0, 0)),
                  pl.BlockSpec((POOL_WIDTH, POOL_WIDTH), lambda b, i: (0, 0)),
                  pl.BlockSpec((1, POOL_WIDTH), lambda b, i: (0, 0))],
        out_specs=pl.BlockSpec((tm, POOL_WIDTH), lambda b, i: (b * nl + i, 0)),
        out_shape=jax.ShapeDtypeStruct((batch * seq_len, POOL_WIDTH), BF16),
        scratch_shapes=[pltpu.VMEM((POOL_TOP + tm, POOL_WIDTH), F32)] * 4,
        compiler_params=_params("parallel", "arbitrary"), name="pool",
    )(u2, u2, hist16, lw["w_pool"], lw["pool_scale"])


def _sigmoid(x):
    return 1.0 / (1.0 + jnp.exp(-x))


def _merge_kernel(x_ref, yp_ref, ym_ref, ys_ref, g1_ref, wg_ref, wbp_ref, wbm_ref, wbs_ref, wo_ref,
                  g2_ref, wr_ref, br_ref, x1_ref, xn2_ref, comb_ref):
    x = x_ref[...]
    xn = (_rms(x) * g1_ref[...]).astype(BF16)
    h = _sigmoid(_dot(xn, wg_ref[:, 0:D_MODEL])) * _dot(yp_ref[...], wbp_ref[...])
    h = h + _sigmoid(_dot(xn, wg_ref[:, D_MODEL:2 * D_MODEL])) * _dot(ym_ref[...], wbm_ref[...])
    h = h + _sigmoid(_dot(xn, wg_ref[:, 2 * D_MODEL:3 * D_MODEL])) * _dot(ys_ref[...], wbs_ref[...])
    x1 = x + _dot(h.astype(BF16), wo_ref[...])
    x1_ref[...] = x1
    xn2 = (_rms(x1) * g2_ref[...]).astype(BF16)
    xn2_ref[...] = xn2
    lg = _dot(xn2, wr_ref[...]) + br_ref[...]
    lane = lax.broadcasted_iota(jnp.int32, lg.shape, 1)
    lanef = lane.astype(F32)
    neg = -jnp.inf
    big = 1e9
    is_g = lane < N_GROUPS
    glog = jnp.where(is_g, lg, neg)
    gmax = jnp.max(glog, axis=1, keepdims=True)
    grp = jnp.min(jnp.where(glog == gmax, lanef, big), axis=1, keepdims=True)
    p_grp = 1.0 / jnp.sum(jnp.where(is_g, jnp.exp(lg - gmax), 0.0), axis=1, keepdims=True)
    eid = lanef - float(ROUTER_EXPERT_LANE0)
    lo = grp * float(EXPERTS_PER_GROUP)
    in_g = (eid >= lo) & (eid < lo + float(EXPERTS_PER_GROUP))
    e1 = jnp.where(in_g, lg, neg)
    v1 = jnp.max(e1, axis=1, keepdims=True)
    i1 = jnp.min(jnp.where(in_g & (e1 == v1), eid, big), axis=1, keepdims=True)
    rest = in_g & (eid != i1)
    e2 = jnp.where(rest, lg, neg)
    v2 = jnp.max(e2, axis=1, keepdims=True)
    i2 = jnp.min(jnp.where(rest & (e2 == v2), eid, big), axis=1, keepdims=True)
    ex = jnp.exp(v2 - v1)
    w1 = (1.0 / (1.0 + ex)) * p_grp
    w2 = (ex / (1.0 + ex)) * p_grp
    comb_ref[...] = jnp.where(eid == i1, w1, 0.0) + jnp.where(eid == i2, w2, 0.0)


def _merge(x2, yp, ym, ys, lw):
    t = x2.shape[0]
    tm = _tile(t, 1024)
    row = lambda i: (i, 0)
    const2 = lambda i: (0, 0)
    resident = pl.Buffered(1)
    return pl.pallas_call(
        _merge_kernel, grid=(t // tm,),
        in_specs=[pl.BlockSpec((tm, D_MODEL), row),
                  pl.BlockSpec((tm, POOL_WIDTH), row),
                  pl.BlockSpec((tm, MLA_HEADS * MLA_V), row),
                  pl.BlockSpec((tm, SB_WIDTH), row),
                  pl.BlockSpec((1, D_MODEL), const2),
                  pl.BlockSpec((D_MODEL, 3 * D_MODEL), const2, pipeline_mode=resident),
                  pl.BlockSpec((POOL_WIDTH, D_MODEL), const2, pipeline_mode=resident),
                  pl.BlockSpec((MLA_HEADS * MLA_V, D_MODEL), const2, pipeline_mode=resident),
                  pl.BlockSpec((SB_WIDTH, D_MODEL), const2, pipeline_mode=resident),
                  pl.BlockSpec((D_MODEL, D_MODEL), const2, pipeline_mode=resident),
                  pl.BlockSpec((1, D_MODEL), const2),
                  pl.BlockSpec((D_MODEL, LANES), const2),
                  pl.BlockSpec((1, LANES), const2)],
        out_specs=(pl.BlockSpec((tm, D_MODEL), row), pl.BlockSpec((tm, D_MODEL), row),
                   pl.BlockSpec((tm, LANES), row)),
        out_shape=(jax.ShapeDtypeStruct((t, D_MODEL), F32), jax.ShapeDtypeStruct((t, D_MODEL), BF16),
                   jax.ShapeDtypeStruct((t, LANES), F32)),
        compiler_params=_params("parallel"), name="merge",
    )(x2, yp, ym, ys, lw["g_mix"], lw["w_gate"], lw["w_br_pool"], lw["w_br_mla"], lw["w_br_sb"],
      lw["w_out"], lw["g_ffn"], lw["w_router"], lw["b_router"])


def _moe_kernel(xn_ref, comb_ref, x1_ref, wg_ref, wu_ref, wd_ref, o_ref):
    xn = xn_ref[...]
    comb = comb_ref[...]
    group_w = EXPERTS_PER_GROUP * D_EXPERT
    y = x1_ref[...]
    for g in range(N_GROUPS):
        cols = slice(g * group_w, (g + 1) * group_w)
        a = _dot(xn, wg_ref[:, cols])
        h = (a * _sigmoid(a)) * _dot(xn, wu_ref[:, cols])
        scaled = []
        for j in range(EXPERTS_PER_GROUP):
            lane = ROUTER_EXPERT_LANE0 + g * EXPERTS_PER_GROUP + j
            scaled.append((h[:, j * D_EXPERT:(j + 1) * D_EXPERT] * comb[:, lane:lane + 1]).astype(BF16))
        y = y + _dot(jnp.concatenate(scaled, axis=1), wd_ref[cols, :])
    o_ref[...] = y


def _moe(xn2, comb, x1, lw):
    t = xn2.shape[0]
    tm = _tile(t, 512)
    row = lambda i: (i, 0)
    const2 = lambda i: (0, 0)
    hidden = N_EXPERTS * D_EXPERT
    resident = pl.Buffered(1)
    return pl.pallas_call(
        _moe_kernel, grid=(t // tm,),
        in_specs=[pl.BlockSpec((tm, D_MODEL), row), pl.BlockSpec((tm, LANES), row),
                  pl.BlockSpec((tm, D_MODEL), row),
                  pl.BlockSpec((D_MODEL, hidden), const2, pipeline_mode=resident),
                  pl.BlockSpec((D_MODEL, hidden), const2, pipeline_mode=resident),
                  pl.BlockSpec((hidden, D_MODEL), const2, pipeline_mode=resident)],
        out_specs=pl.BlockSpec((tm, D_MODEL), row),
        out_shape=jax.ShapeDtypeStruct((t, D_MODEL), F32),
        compiler_params=_params("parallel"), name="moe",
    )(xn2, comb, x1, lw["w_exp_gate"], lw["w_exp_up"], lw["w_exp_down"])


def _head_blocks(w, n_heads, width, place):
    k = w.shape[0]
    blocks = [place(h, w[:, h * width:(h + 1) * width]) for h in range(n_heads)]
    return jnp.concatenate(blocks, axis=1).reshape(k, n_heads * LANES)


def _pad_lanes(w, before, total=LANES):
    return jnp.pad(w, ((0, 0), (before, total - before - w.shape[1])))


def _swap_halves(w):
    half = w.shape[1] // 2
    return jnp.concatenate([-w[:, half:], w[:, :half]], axis=1)


def _experts_side_by_side(w):
    e, d, f = w.shape
    return jnp.transpose(w.astype(BF16), (1, 0, 2)).reshape(d, e * f)


def _prep_layer(p):
    w_in = p["w_in"].astype(BF16)
    w_uq = p["w_uq"].astype(BF16)
    w_ukv = p["w_ukv"].astype(BF16)
    w_kr = w_in[:, 640:672]
    w_cat = jnp.concatenate(
        [w_in[:, 0:640], w_in[:, 672:1440],
         _pad_lanes(w_kr, MLA_NOPE), _pad_lanes(_swap_halves(w_kr), MLA_NOPE)], axis=1)
    wq = _head_blocks(w_uq, MLA_HEADS, MLA_QK, lambda h, b: _pad_lanes(b, 0))
    wqs = _head_blocks(w_uq, MLA_HEADS, MLA_QK,
                       lambda h, b: _pad_lanes(_swap_halves(b[:, MLA_NOPE:]), MLA_NOPE))
    wk = _head_blocks(w_ukv, MLA_HEADS, MLA_NOPE + MLA_V,
                      lambda h, b: _pad_lanes(b[:, :MLA_NOPE], 0))
    wv = _head_blocks(w_ukv, MLA_HEADS, MLA_NOPE + MLA_V,
                      lambda h, b: _pad_lanes(b[:, MLA_NOPE:], (h % 2) * MLA_V))
    w_pool = jax.scipy.linalg.block_diag(*[p["w_pool_lin"][g] for g in range(4)])
    w_router = _pad_lanes(jnp.concatenate([p["w_router_group"], p["w_router_expert"]], axis=1), 0)
    b_router = _pad_lanes(jnp.concatenate([p["b_router_group"], p["b_router_expert"]])[None, :], 0)
    return {
        "g_mix": p["g_mix_norm"][None, :],
        "w_cat": w_cat.astype(BF16),
        "g_q_lora": p["g_q_lora"][None, :],
        "wq": wq.astype(BF16), "wqs": wqs.astype(BF16),
        "g_qk_q": _pad_lanes(p["g_qk_q"][None, :], 0),
        "g_kv": p["g_kv_lora"][None, :],
        "wk": wk.astype(BF16), "wv": wv.astype(BF16),
        "g_qk_k": _pad_lanes(p["g_qk_k"][None, :], 0),
        "mla_bound": ((MLA_QK ** 0.5) * LOG2_E * MLA_BOUND_SLACK
                      * jnp.max(jnp.abs(p["g_qk_q"])) * jnp.max(jnp.abs(p["g_qk_k"]))).reshape(1),
        "w_pool": w_pool.astype(BF16),
        "pool_scale": p["pool_scale"][None, :],
        "w_gate": w_in[:, 1440:].astype(BF16),
        "w_br_pool": p["w_br_pool"].astype(BF16),
        "w_br_mla": p["w_br_mla"].astype(BF16),
        "w_br_sb": p["w_br_sb"].astype(BF16),
        "w_out": p["w_out"].astype(BF16),
        "g_ffn": p["g_ffn_norm"][None, :],
        "w_router": w_router.astype(BF16),
        "b_router": b_router,
        "w_exp_gate": _experts_side_by_side(p["w_exp_gate"]),
        "w_exp_up": _experts_side_by_side(p["w_exp_up"]),
        "w_exp_down": p["w_exp_down"].astype(BF16).reshape(N_EXPERTS * D_EXPERT, D_MODEL),
    }


def _rope_table(pos):
    half = MLA_ROPE // 2
    inv = ROPE_THETA ** (-jnp.arange(half, dtype=F32) / half)
    ang = pos.astype(F32)[:, None] * inv[None, :]
    cos, sin = jnp.cos(ang), jnp.sin(ang)
    n = pos.shape[0]
    cos_t = jnp.concatenate([jnp.ones((n, MLA_NOPE), F32), cos, cos, jnp.zeros((n, LANES - MLA_QK), F32)], axis=1)
    sin_t = jnp.concatenate([jnp.zeros((n, MLA_NOPE), F32), sin, sin, jnp.zeros((n, LANES - MLA_QK), F32)], axis=1)
    return jnp.stack([cos_t, sin_t])


def _rows_into(rows, old, new, lane0=0, width=None):
    b, n_old, w_old = old.shape
    width = w_old if width is None else width
    old = jnp.pad(old.astype(new.dtype), ((0, 0), (0, 0), (lane0, width - lane0 - w_old)))
    new = new.reshape(b, -1, width)
    tail = jnp.zeros((b, rows - n_old - new.shape[1], width), new.dtype)
    return jnp.concatenate([old, new, tail], axis=1)


def _layer(x, cs, lw, hist, past):
    b, seq_len, _ = x.shape
    t = b * seq_len
    x2 = x.reshape(t, D_MODEL)
    up, q, lat, kr, qsb, ksb, vsb, ksbh, vsbh, kr_new, k_mla, v_mla = _inproj(x2, cs, lw, seq_len)
    if hist is None:
        lk = seq_len
        kv_len = seq_len
        ksb_all, vsb_all = ksbh, vsbh
        hist16 = jnp.zeros((b, POOL_HALO, POOL_WIDTH), F32)
    else:
        kv_len = past + seq_len
        lk = -(-kv_len // 256) * 256
        lat_all = _rows_into(lk, hist["lat"], lat).reshape(b * lk, MLA_KV_LORA)
        kr_all = _rows_into(lk, hist["kr"], kr, MLA_NOPE, LANES).reshape(b * lk, LANES)
        ksb_all = _rows_into(lk, hist["sbk"].reshape(b, past, SB_WIDTH), ksbh).reshape(b * lk, SB_WIDTH)
        vsb_all = _rows_into(lk, hist["sbv"].reshape(b, past, SB_WIDTH), vsbh).reshape(b * lk, SB_WIDTH)
        hist16 = jnp.pad(hist["pool"], ((0, 0), (POOL_HALO - POOL_HIST, 0), (0, 0)))
        k_mla, v_mla = _mla_keys(lat_all, kr_all, lw)
    y_mla = _mla_attn(lw["mla_bound"], q, k_mla, v_mla, b, seq_len, lk, past, kv_len)
    y_sb = _sb_attn(qsb, ksb_all, vsb_all, b, seq_len, lk, past)
    y_pool = _pool(up, hist16, lw, b, seq_len, past)
    x1, xn2, comb = _merge(x2, y_pool, y_mla, y_sb, lw)
    x_out = _moe(xn2, comb, x1, lw).reshape(b, seq_len, D_MODEL)
    up3 = up.reshape(b, seq_len, POOL_WIDTH)
    if seq_len >= POOL_HIST:
        pool_state = up3[:, seq_len - POOL_HIST:]
    else:
        pool_state = jnp.concatenate([hist16[:, 1:], up3], axis=1)[:, -POOL_HIST:]
    state = (lat.reshape(b, seq_len, MLA_KV_LORA),
             kr_new.reshape(b, seq_len, MLA_ROPE),
             ksb.reshape(b, seq_len, SB_HEADS, SB_HEAD_DIM),
             vsb.reshape(b, seq_len, SB_HEADS, SB_HEAD_DIM),
             pool_state)
    return x_out, state


def kernel(x_prompt, x_sample, cache_mla_latent, cache_mla_krope, cache_sb_k, cache_sb_v, state_pool,
           g_mix_norm, w_in, w_pool_lin, pool_scale, g_q_lora, w_uq, g_kv_lora, w_ukv, g_qk_q, g_qk_k,
           w_br_pool, w_br_mla, w_br_sb, w_out, g_ffn_norm, w_router_group, b_router_group,
           w_router_expert, b_router_expert, w_exp_gate, w_exp_up, w_exp_down):
    depth = w_in.shape[0]
    past = cache_mla_latent.shape[2]
    cs_p = _rope_table(jnp.arange(x_prompt.shape[1], dtype=jnp.int32))
    cs_s = _rope_table(past + jnp.arange(x_sample.shape[1], dtype=jnp.int32))
    xp, xs = x_prompt, x_sample
    new_p = [[] for _ in range(5)]
    new_s = [[] for _ in range(5)]
    for l in range(depth):
        lw = _prep_layer({
            "g_mix_norm": g_mix_norm[l], "w_in": w_in[l], "w_pool_lin": w_pool_lin[l],
            "pool_scale": pool_scale[l], "g_q_lora": g_q_lora[l], "w_uq": w_uq[l],
            "g_kv_lora": g_kv_lora[l], "w_ukv": w_ukv[l], "g_qk_q": g_qk_q[l], "g_qk_k": g_qk_k[l],
            "w_br_pool": w_br_pool[l], "w_br_mla": w_br_mla[l], "w_br_sb": w_br_sb[l],
            "w_out": w_out[l], "g_ffn_norm": g_ffn_norm[l],
            "w_router_group": w_router_group[l], "b_router_group": b_router_group[l],
            "w_router_expert": w_router_expert[l], "b_router_expert": b_router_expert[l],
            "w_exp_gate": w_exp_gate[l], "w_exp_up": w_exp_up[l], "w_exp_down": w_exp_down[l],
        })
        hist = {"lat": cache_mla_latent[l], "kr": cache_mla_krope[l], "sbk": cache_sb_k[l],
                "sbv": cache_sb_v[l], "pool": state_pool[l]}
        xp, st_p = _layer(xp, cs_p, lw, None, 0)
        xs, st_s = _layer(xs, cs_s, lw, hist, past)
        for i in range(5):
            new_p[i].append(st_p[i])
            new_s[i].append(st_s[i])
    return (xp, xs, *[jnp.stack(s) for s in new_p], *[jnp.stack(s) for s in new_s])
```

```python
import functools

import jax
import jax.numpy as jnp
from jax import lax
from jax.experimental import pallas as pl
from jax.experimental.pallas import tpu as pltpu

F32 = jnp.float32
BF16 = jnp.bfloat16

D_MODEL = 1024
CHUNK_SHIFT = 6
EPS = 1e-6
POOL_WIDTH = 256
POOL_HIST = 15
POOL_HALO = 16
POOL_TOP = 32
MLA_HEADS = 8
MLA_Q_LORA = 256
MLA_KV_LORA = 128
MLA_NOPE = 64
MLA_ROPE = 32
MLA_V = 64
MLA_QK = MLA_NOPE + MLA_ROPE
ROPE_THETA = 10000.0
SB_HEADS = 4
SB_HEAD_DIM = 64
SB_WIDTH = 256
N_GROUPS = 4
EXPERTS_PER_GROUP = 4
N_EXPERTS = 16
D_EXPERT = 256
LANES = 128
ROUTER_EXPERT_LANE0 = N_GROUPS
VMEM_LIMIT = 56 * 1024 * 1024
NEG_BIG = -1e30
LOG2_E = 1.4426950408889634
SB_SUB = 256
ATTN_TILE_ELEMS = 512 * 512
MLA_STEP_HEADS = 8
MLA_BOUND_SLACK = 1.02
MLA_SAFE_BOUND = 60.0

C_UPOOL = 0
C_CQ = 256
C_CKV = 512
C_QSB = 640
C_KSB = 896
C_VSB = 1152
C_KRA = 1408
C_KRB = 1536
C_TOTAL = 1664


def _tile(n, pref, mult=16):
    best = None
    for t in range(mult, min(n, pref) + 1, mult):
        if n % t == 0:
            best = t
    return best if best is not None else n


def _params(*sem, flags=None):
    return pltpu.CompilerParams(dimension_semantics=sem, vmem_limit_bytes=VMEM_LIMIT, flags=flags)


def _rms(x, inv_n=None):
    if inv_n is None:
        ms = jnp.mean(x * x, axis=-1, keepdims=True)
    else:
        ms = jnp.sum(x * x, axis=-1, keepdims=True) * inv_n
    return x * lax.rsqrt(ms + EPS)


def _dot(a, b):
    return jnp.dot(a, b, preferred_element_type=F32)


def _dot_nt(a, b):
    return lax.dot_general(a, b, (((1,), (1,)), ((), ())), preferred_element_type=F32)


def _inproj_kernel(x_ref, g_ref, w_ref, cs_ref, gq_ref, wq_ref, wqs_ref, gqk_ref, gkv_ref, wk_ref, wv_ref, gqkk_ref,
                   up_ref, q_ref, lat_ref, kr_ref, qsb_ref, ksb_ref, vsb_ref, ksbh_ref, vsbh_ref, krc_ref,
                   k_ref, v_ref):
    x = x_ref[...]
    xn = (_rms(x) * g_ref[...]).astype(BF16)
    z = _dot(xn, w_ref[...])
    up_ref[...] = z[:, C_UPOOL:C_UPOOL + POOL_WIDTH]
    cq = z[:, C_CQ:C_CQ + MLA_Q_LORA]
    cqn = (_rms(cq) * gq_ref[...]).astype(BF16)
    qa = _dot(cqn, wq_ref[...])
    qb = _dot(cqn, wqs_ref[...])
    cos = cs_ref[0]
    sin = cs_ref[1]
    gqk = gqk_ref[...]
    for h in range(MLA_HEADS):
        sl = slice(h * LANES, (h + 1) * LANES)
        qh = qa[:, sl] * cos + qb[:, sl] * sin
        q_ref[h] = (_rms(qh, 1.0 / MLA_QK) * gqk).astype(BF16)
    ckv = z[:, C_CKV:C_CKV + MLA_KV_LORA]
    lat = _rms(ckv) * gkv_ref[...]
    lat_ref[...] = lat
    kr = z[:, C_KRA:C_KRA + LANES] * cos + z[:, C_KRB:C_KRB + LANES] * sin
    kr_ref[...] = kr
    krc_ref[...] = kr[:, MLA_NOPE:MLA_QK]
    _store_mla_keys(lat, kr, wk_ref, wv_ref, gqkk_ref, k_ref, v_ref)
    qsb_ref[...] = (z[:, C_QSB:C_QSB + SB_WIDTH] * (SB_HEAD_DIM ** -0.5)).astype(BF16)
    ksb = z[:, C_KSB:C_KSB + SB_WIDTH]
    ksb_ref[...] = ksb
    ksbh_ref[...] = ksb.astype(BF16)
    vsb = z[:, C_VSB:C_VSB + SB_WIDTH]
    vsb_ref[...] = vsb
    vsbh_ref[...] = vsb.astype(BF16)


def _inproj(x2, cs, lw, seq_len):
    t = x2.shape[0]
    tm = _tile(t, 1024)
    if tm <= seq_len and seq_len % tm == 0:
        per = seq_len // tm
        cs_map = lambda i: (0, i % per, 0)
    else:
        assert tm % seq_len == 0
        cs = jnp.tile(cs, (1, tm // seq_len, 1))
        cs_map = lambda i: (0, 0, 0)
    row = lambda i: (i, 0)
    const2 = lambda i: (0, 0)
    out_shape = (
        jax.ShapeDtypeStruct((t, POOL_WIDTH), F32),
        jax.ShapeDtypeStruct((MLA_HEADS, t, LANES), BF16),
        jax.ShapeDtypeStruct((t, MLA_KV_LORA), F32),
        jax.ShapeDtypeStruct((t, LANES), F32),
        jax.ShapeDtypeStruct((t, SB_WIDTH), BF16),
        jax.ShapeDtypeStruct((t, SB_WIDTH), F32),
        jax.ShapeDtypeStruct((t, SB_WIDTH), F32),
        jax.ShapeDtypeStruct((t, SB_WIDTH), BF16),
        jax.ShapeDtypeStruct((t, SB_WIDTH), BF16),
        jax.ShapeDtypeStruct((t, MLA_ROPE), F32),
        jax.ShapeDtypeStruct((MLA_HEADS, t, LANES), BF16),
        jax.ShapeDtypeStruct((MLA_HEADS, t, LANES), BF16),
    )
    out_specs = (
        pl.BlockSpec((tm, POOL_WIDTH), row),
        pl.BlockSpec((MLA_HEADS, tm, LANES), lambda i: (0, i, 0)),
        pl.BlockSpec((tm, MLA_KV_LORA), row),
        pl.BlockSpec((tm, LANES), row),
        pl.BlockSpec((tm, SB_WIDTH), row),
        pl.BlockSpec((tm, SB_WIDTH), row),
        pl.BlockSpec((tm, SB_WIDTH), row),
        pl.BlockSpec((tm, SB_WIDTH), row),
        pl.BlockSpec((tm, SB_WIDTH), row),
        pl.BlockSpec((tm, MLA_ROPE), row),
        pl.BlockSpec((MLA_HEADS, tm, LANES), lambda i: (0, i, 0)),
        pl.BlockSpec((MLA_HEADS, tm, LANES), lambda i: (0, i, 0)),
    )
    in_specs = [
        pl.BlockSpec((tm, D_MODEL), row),
        pl.BlockSpec((1, D_MODEL), const2),
        pl.BlockSpec((D_MODEL, C_TOTAL), const2),
        pl.BlockSpec((2, tm, LANES), cs_map),
        pl.BlockSpec((1, MLA_Q_LORA), const2),
        pl.BlockSpec((MLA_Q_LORA, MLA_HEADS * LANES), const2),
        pl.BlockSpec((MLA_Q_LORA, MLA_HEADS * LANES), const2),
        pl.BlockSpec((1, LANES), const2),
        pl.BlockSpec((1, MLA_KV_LORA), const2),
        pl.BlockSpec((MLA_KV_LORA, MLA_HEADS * LANES), const2),
        pl.BlockSpec((MLA_KV_LORA, MLA_HEADS * LANES), const2),
        pl.BlockSpec((1, LANES), const2),
    ]
    return pl.pallas_call(
        _inproj_kernel, grid=(t // tm,), in_specs=in_specs, out_specs=out_specs, out_shape=out_shape,
        compiler_params=_params("parallel"), name="inproj",
    )(x2, lw["g_mix"], lw["w_cat"], cs, lw["g_q_lora"], lw["wq"], lw["wqs"], lw["g_qk_q"], lw["g_kv"],
      lw["wk"], lw["wv"], lw["g_qk_k"])


def _mla_keys_kernel(lat_ref, kr_ref, wk_ref, wv_ref, g_ref, k_ref, v_ref):
    _store_mla_keys(lat_ref[...], kr_ref[...], wk_ref, wv_ref, g_ref, k_ref, v_ref)


def _store_mla_keys(lat, kr, wk_ref, wv_ref, g_ref, k_ref, v_ref):
    latb = lat.astype(BF16)
    kn = _dot(latb, wk_ref[...])
    vv = _dot(latb, wv_ref[...])
    g = g_ref[...]
    lane = lax.broadcasted_iota(jnp.int32, (1, LANES), 1)
    for h in range(MLA_HEADS):
        sl = slice(h * LANES, (h + 1) * LANES)
        kh = kn[:, sl] + kr
        k_ref[h] = (_rms(kh, 1.0 / MLA_QK) * g).astype(BF16)
        ones = (lane == _mla_ones_lane(h)).astype(F32)
        v_ref[h] = (vv[:, sl] + ones).astype(BF16)


def _mla_keys(lat2, kr2, lw):
    t = lat2.shape[0]
    tm = _tile(t, 1024)
    row = lambda i: (i, 0)
    const2 = lambda i: (0, 0)
    hd = lambda i: (0, i, 0)
    shp = jax.ShapeDtypeStruct((MLA_HEADS, t, LANES), BF16)
    return pl.pallas_call(
        _mla_keys_kernel, grid=(t // tm,),
        in_specs=[pl.BlockSpec((tm, MLA_KV_LORA), row), pl.BlockSpec((tm, LANES), row),
                  pl.BlockSpec((MLA_KV_LORA, MLA_HEADS * LANES), const2),
                  pl.BlockSpec((MLA_KV_LORA, MLA_HEADS * LANES), const2),
                  pl.BlockSpec((1, LANES), const2)],
        out_specs=(pl.BlockSpec((MLA_HEADS, tm, LANES), hd), pl.BlockSpec((MLA_HEADS, tm, LANES), hd)),
        out_shape=(shp, shp), compiler_params=_params("parallel"), name="mla_keys",
    )(lat2, kr2, lw["wk"], lw["wv"], lw["g_qk_k"])


def _aligned_diagonal(tq, tk, q_off, lq, kv_len):
    return tq == tk and q_off % tq == 0 and tq % (2 * LANES) == 0 and kv_len >= q_off + lq


def _mla_ones_lane(h):
    return MLA_V if h % 2 == 0 else 0


def _mla_attn_kernel(bound_ref, q_ref, k_ref, v_ref, o_ref, m_ref, acc_ref, *, tq, tk, q_off, kv_len,
                     quartered):
    i = pl.program_id(2)
    q0 = q_off + i * tq
    vis_all = jnp.minimum(((q0 >> CHUNK_SHIFT) + 1) << CHUNK_SHIFT, kv_len)
    vis_any = jnp.minimum((((q0 + tq - 1) >> CHUNK_SHIFT) + 1) << CHUNK_SHIFT, kv_len)
    n_full = vis_all // tk
    n_blk = (vis_any + tk - 1) // tk
    c = (MLA_QK ** -0.5) * LOG2_E
    bound = bound_ref[0]
    acc_ref[...] = jnp.zeros(acc_ref.shape, F32)

    def block(rows, start, width, ok, bounded):
        for hh in range(MLA_STEP_HEADS):
            s = _dot_nt(q_ref[hh, rows], k_ref[hh, pl.ds(start, width), :])
            if ok is not None:
                s = jnp.where(ok, s, -jnp.inf)
            v = v_ref[hh, pl.ds(start, width), :]
            if bounded:
                acc_ref[hh, rows] += _dot(jnp.exp2(s * c - bound).astype(BF16), v)
            else:
                m_prev = m_ref[hh, rows]
                m_new = jnp.maximum(m_prev, jnp.max(s, axis=1, keepdims=True))
                alpha = jnp.exp2((m_prev - m_new) * c)
                p = jnp.exp2((s - jnp.tile(m_new, (1, width // LANES))) * c)
                acc_ref[hh, rows] = alpha * acc_ref[hh, rows] + _dot(p.astype(BF16), v)
                m_ref[hh, rows] = m_new

    def step(j, masked, bounded):
        start = pl.multiple_of(j * tk, tk)
        ok = None
        if masked:
            qpos = q0 + lax.broadcasted_iota(jnp.int32, (tq, tk), 0)
            kpos = start + lax.broadcasted_iota(jnp.int32, (tq, tk), 1)
            ok = ((kpos >> CHUNK_SHIFT) <= (qpos >> CHUNK_SHIFT)) & (kpos < kv_len)
        block(slice(0, tq), start, tk, ok, bounded)

    def diagonal(bounded):
        half = tq // 2
        ok = ((lax.broadcasted_iota(jnp.int32, (half, half), 1) >> CHUNK_SHIFT) <=
              (lax.broadcasted_iota(jnp.int32, (half, half), 0) >> CHUNK_SHIFT))
        start = pl.multiple_of(q0, tq)
        block(slice(0, half), start, half, ok, bounded)
        block(slice(half, tq), start, half, None, bounded)
        block(slice(half, tq), pl.multiple_of(q0 + half, half), half, ok, bounded)

    def loops(bounded):
        lax.fori_loop(0, n_full, lambda j, carry: (step(j, False, bounded), carry)[1], 0)
        if quartered:
            diagonal(bounded)
        else:
            lax.fori_loop(n_full, n_blk, lambda j, carry: (step(j, True, bounded), carry)[1], 0)

    @pl.when(bound <= MLA_SAFE_BOUND)
    def _():
        loops(True)

    @pl.when(bound > MLA_SAFE_BOUND)
    def _():
        m_ref[...] = jnp.full(m_ref.shape, NEG_BIG, F32)
        loops(False)

    lane = lax.broadcasted_iota(jnp.int32, (1, LANES), 1)
    for pair in range(MLA_STEP_HEADS // 2):
        outs = []
        for hh in (2 * pair, 2 * pair + 1):
            acc = acc_ref[hh]
            denom = jnp.sum(jnp.where(lane == _mla_ones_lane(hh), acc, 0.0), axis=1, keepdims=True)
            outs.append(acc / denom)
        o_ref[:, pair * LANES:(pair + 1) * LANES] = jnp.where(lane < MLA_V, outs[0], outs[1]).astype(BF16)


def _mla_attn(bound, q, k, v, batch, lq, lk, q_off, kv_len):
    tq = _tile(lq, 512)
    tk = _tile(lk, ATTN_TILE_ELEMS // tq, LANES)
    assert tk % LANES == 0
    nq = lq // tq
    nh = MLA_STEP_HEADS
    kern = functools.partial(_mla_attn_kernel, tq=tq, tk=tk, q_off=q_off, kv_len=kv_len,
                             quartered=_aligned_diagonal(tq, tk, q_off, lq, kv_len))
    return pl.pallas_call(
        kern, grid=(batch, MLA_HEADS // nh, nq),
        in_specs=[pl.BlockSpec(memory_space=pltpu.SMEM),
                  pl.BlockSpec((nh, tq, LANES), lambda b, p, i: (p, b * nq + i, 0)),
                  pl.BlockSpec((nh, lk, LANES), lambda b, p, i: (p, b, 0)),
                  pl.BlockSpec((nh, lk, LANES), lambda b, p, i: (p, b, 0))],
        out_specs=pl.BlockSpec((tq, nh * MLA_V), lambda b, p, i: (b * nq + i, p)),
        out_shape=jax.ShapeDtypeStruct((batch * lq, MLA_HEADS * MLA_V), BF16),
        scratch_shapes=[pltpu.VMEM((nh, tq, LANES), F32), pltpu.VMEM((nh, tq, LANES), F32)],
        compiler_params=_params("parallel", "parallel", "arbitrary"), name="mla_attn",
    )(bound, q, k, v)


def _sb_attn_kernel(q_ref, k_ref, v_ref, o_ref, c_ref, acc_ref, *, tq, tk, q_off, quartered):
    i = pl.program_id(2)
    q0 = q_off + i * tq
    n_full = q0 // tk
    n_blk = (q0 + tq - 2) // tk + 1
    lane = lax.broadcasted_iota(jnp.int32, (1, LANES), 1)
    upper = (lax.broadcasted_iota(jnp.int32, (SB_SUB, SB_SUB), 0) >
             lax.broadcasted_iota(jnp.int32, (SB_SUB, SB_SUB), 1)).astype(BF16)
    acc_ref[...] = jnp.zeros(acc_ref.shape, F32)
    c_ref[...] = jnp.zeros(c_ref.shape, F32)
    zero = jnp.zeros((), BF16)
    pair_lanes = [slice((hh // 2) * LANES, (hh // 2 + 1) * LANES) for hh in range(SB_HEADS)]
    in_head = [(lane >= (hh % 2) * SB_HEAD_DIM) & (lane < (hh % 2 + 1) * SB_HEAD_DIM) for hh in range(SB_HEADS)]
    qs = [jnp.where(in_head[hh], q_ref[:, pair_lanes[hh]], zero) for hh in range(SB_HEADS)]

    def block(rows, start, width, ok):
        for hh in range(SB_HEADS):
            k = k_ref[pl.ds(start, width), pair_lanes[hh]]
            v = v_ref[pl.ds(start, width), pair_lanes[hh]]
            z = _dot_nt(qs[hh][rows], k)
            t = jnp.log(1.0 + jnp.exp2(jnp.abs(z) * (-LOG2_E)))
            log_beta = jnp.minimum(z, 0.0) - t
            log_keep = log_beta - z
            if ok is not None:
                log_keep = jnp.where(ok, log_keep, 0.0)
            keep16 = log_keep.astype(BF16)
            parts = []
            later = None
            for sb in reversed(range(width // SB_SUB)):
                sl = slice(sb * SB_SUB, (sb + 1) * SB_SUB)
                inside = _dot(keep16[:, sl], upper)
                if later is not None:
                    inside = inside + later
                parts.append(jnp.exp(log_beta[:, sl] + inside))
                total = jnp.sum(log_keep[:, sl], axis=1, keepdims=True)
                later = total if later is None else later + total
            a = parts[0] if len(parts) == 1 else jnp.concatenate(parts[::-1], axis=1)
            if ok is not None:
                a = jnp.where(ok, a, 0.0)
            before = c_ref[hh, rows]
            pv = _dot(a.astype(BF16), jnp.where(in_head[hh], v, zero))
            acc_ref[rows, pair_lanes[hh]] += jnp.exp(before) * pv
            c_ref[hh, rows] = before + later

    def step(j, masked):
        start = pl.multiple_of(j * tk, tk)
        ok = None
        if masked:
            qpos = q0 + lax.broadcasted_iota(jnp.int32, (tq, tk), 0)
            kpos = start + lax.broadcasted_iota(jnp.int32, (tq, tk), 1)
            ok = kpos < qpos
        block(slice(0, tq), start, tk, ok)

    def masked_body(it, carry):
        step(n_blk - 1 - it, True)
        return carry

    def full_body(it, carry):
        step(n_full - 1 - it, False)
        return carry

    if quartered:
        half = tq // 2
        ok = (lax.broadcasted_iota(jnp.int32, (half, half), 1) <
              lax.broadcasted_iota(jnp.int32, (half, half), 0))
        start = pl.multiple_of(q0, tq)
        block(slice(half, tq), pl.multiple_of(q0 + half, half), half, ok)
        block(slice(half, tq), start, half, None)
        block(slice(0, half), start, half, ok)
    else:
        lax.fori_loop(0, n_blk - n_full, masked_body, 0)
    lax.fori_loop(0, n_full, full_body, 0)
    o_ref[...] = acc_ref[...].astype(BF16)


def _sb_attn(q, k, v, batch, lq, lk, q_off):
    tq = _tile(lq, 512)
    tk = _tile(lk, ATTN_TILE_ELEMS // tq, SB_SUB)
    assert tk % SB_SUB == 0
    nq = lq // tq
    quartered = _aligned_diagonal(tq, tk, q_off, lq, lk) and (tq // 2) % SB_SUB == 0
    kern = functools.partial(_sb_attn_kernel, tq=tq, tk=tk, q_off=q_off, quartered=quartered)
    return pl.pallas_call(
        kern, grid=(batch, 1, nq),
        in_specs=[pl.BlockSpec((tq, SB_WIDTH), lambda b, p, i: (b * nq + i, 0)),
                  pl.BlockSpec((lk, SB_WIDTH), lambda b, p, i: (b, 0)),
                  pl.BlockSpec((lk, SB_WIDTH), lambda b, p, i: (b, 0))],
        out_specs=pl.BlockSpec((tq, SB_WIDTH), lambda b, p, i: (b * nq + i, 0)),
        out_shape=jax.ShapeDtypeStruct((batch * lq, SB_WIDTH), BF16),
        scratch_shapes=[pltpu.VMEM((SB_HEADS, tq, LANES), F32), pltpu.VMEM((tq, SB_WIDTH), F32)],
        compiler_params=_params("parallel", "parallel", "arbitrary"), name="sb_attn",
    )(q, k, v)


def _pool_kernel(u_ref, halo_ref, hist_ref, w_ref, sc_ref, y_ref, s0, s1, s2, s3, *, tm, past):
    i = pl.program_id(1)
    top = POOL_TOP
    end = top + tm
    s0[0:top - POOL_HALO, :] = jnp.zeros((top - POOL_HALO, POOL_WIDTH), F32)
    s0[top:end, :] = u_ref[...]

    @pl.when(i == 0)
    def _():
        s0[top - POOL_HALO:top, :] = hist_ref[0]

    @pl.when(i > 0)
    def _():
        s0[top - POOL_HALO:top, :] = halo_ref[...]

    s1[8:end, :] = s0[8:end, :] + s0[7:end - 1, :]
    s2[16:end, :] = s1[16:end, :] + s1[14:end - 2, :]
    s3[24:end, :] = s2[24:end, :] + s2[20:end - 4, :]
    x0 = s0[top:end, :]
    a2 = s1[top:end, :]
    a4 = s2[top:end, :]
    a8 = s3[top:end, :]
    a16 = a8 + s3[top - 8:end - 8, :]
    grp = lax.broadcasted_iota(jnp.int32, (1, POOL_WIDTH), 1) >> 6
    win = jnp.where(grp == 0, a2, jnp.where(grp == 1, a4, jnp.where(grp == 2, a8, a16)))
    width = jnp.where(grp == 0, 2.0, jnp.where(grp == 1, 4.0, jnp.where(grp == 2, 8.0, 16.0)))
    pos = past + i * tm + lax.broadcasted_iota(jnp.int32, (tm, 1), 0)
    cnt = jnp.minimum((pos + 1).astype(F32), width)
    d = win / cnt - x0
    y_ref[...] = (_dot(d.astype(BF16), w_ref[...]) * sc_ref[...]).astype(BF16)


def _pool(u2, hist16, lw, batch, seq_len, past):
    tm = _tile(seq_len, 2048)
    nl = seq_len // tm
    per = tm // POOL_HALO
    kern = functools.partial(_pool_kernel, tm=tm, past=past)
    return pl.pallas_call(
        kern, grid=(batch, nl),
        in_specs=[pl.BlockSpec((tm, POOL_WIDTH), lambda b, i: (b * nl + i, 0)),
                  pl.BlockSpec((POOL_HALO, POOL_WIDTH),
                               lambda b, i: (jnp.maximum((b * nl + i) * per - 1, 0), 0)),
                  pl.BlockSpec((1, POOL_HALO, POOL_WIDTH), lambda b, i: (b, 0, 0)),
                  pl.BlockSpec((POOL_WIDTH, POOL_WIDTH), lambda b, i: (0, 0)),
                  pl.BlockSpec((1, POOL_WIDTH), lambda b, i: (0, 0))],
        out_specs=pl.BlockSpec((tm, POOL_WIDTH), lambda b, i: (b * nl + i, 0)),
        out_shape=jax.ShapeDtypeStruct((batch * seq_len, POOL_WIDTH), BF16),
        scratch_shapes=[pltpu.VMEM((POOL_TOP + tm, POOL_WIDTH), F32)] * 4,
        compiler_params=_params("parallel", "arbitrary"), name="pool",
    )(u2, u2, hist16, lw["w_pool"], lw["pool_scale"])


def _sigmoid(x):
    return 1.0 / (1.0 + jnp.exp(-x))


def _merge_kernel(x_ref, yp_ref, ym_ref, ys_ref, g1_ref, wg_ref, wbp_ref, wbm_ref, wbs_ref, wo_ref,
                  g2_ref, wr_ref, br_ref, x1_ref, xn2_ref, comb_ref):
    x = x_ref[...]
    xn = (_rms(x) * g1_ref[...]).astype(BF16)
    h = _sigmoid(_dot(xn, wg_ref[:, 0:D_MODEL])) * _dot(yp_ref[...], wbp_ref[...])
    h = h + _sigmoid(_dot(xn, wg_ref[:, D_MODEL:2 * D_MODEL])) * _dot(ym_ref[...], wbm_ref[...])
    h = h + _sigmoid(_dot(xn, wg_ref[:, 2 * D_MODEL:3 * D_MODEL])) * _dot(ys_ref[...], wbs_ref[...])
    x1 = x + _dot(h.astype(BF16), wo_ref[...])
    x1_ref[...] = x1
    xn2 = (_rms(x1) * g2_ref[...]).astype(BF16)
    xn2_ref[...] = xn2
    comb_ref[...] = _dot(xn2, wr_ref[...]) + br_ref[...]


def _route(lg):
    lane = lax.broadcasted_iota(jnp.int32, lg.shape, 1)
    lanef = lane.astype(F32)
    neg = -jnp.inf
    big = 1e9
    is_g = lane < N_GROUPS
    glog = jnp.where(is_g, lg, neg)
    gmax = jnp.max(glog, axis=1, keepdims=True)
    grp = jnp.min(jnp.where(glog == gmax, lanef, big), axis=1, keepdims=True)
    p_grp = 1.0 / jnp.sum(jnp.where(is_g, jnp.exp(lg - gmax), 0.0), axis=1, keepdims=True)
    eid = lanef - float(ROUTER_EXPERT_LANE0)
    lo = grp * float(EXPERTS_PER_GROUP)
    in_g = (eid >= lo) & (eid < lo + float(EXPERTS_PER_GROUP))
    e1 = jnp.where(in_g, lg, neg)
    v1 = jnp.max(e1, axis=1, keepdims=True)
    i1 = jnp.min(jnp.where(in_g & (e1 == v1), eid, big), axis=1, keepdims=True)
    rest = in_g & (eid != i1)
    e2 = jnp.where(rest, lg, neg)
    v2 = jnp.max(e2, axis=1, keepdims=True)
    i2 = jnp.min(jnp.where(rest & (e2 == v2), eid, big), axis=1, keepdims=True)
    ex = jnp.exp(v2 - v1)
    w1 = (1.0 / (1.0 + ex)) * p_grp
    w2 = (ex / (1.0 + ex)) * p_grp
    return jnp.where(eid == i1, w1, 0.0) + jnp.where(eid == i2, w2, 0.0)


def _merge(x2, yp, ym, ys, lw):
    t = x2.shape[0]
    tm = _tile(t, 1024)
    row = lambda i: (i, 0)
    const2 = lambda i: (0, 0)
    resident = pl.Buffered(1)
    return pl.pallas_call(
        _merge_kernel, grid=(t // tm,),
        in_specs=[pl.BlockSpec((tm, D_MODEL), row),
                  pl.BlockSpec((tm, POOL_WIDTH), row),
                  pl.BlockSpec((tm, MLA_HEADS * MLA_V), row),
                  pl.BlockSpec((tm, SB_WIDTH), row),
                  pl.BlockSpec((1, D_MODEL), const2),
                  pl.BlockSpec((D_MODEL, 3 * D_MODEL), const2, pipeline_mode=resident),
                  pl.BlockSpec((POOL_WIDTH, D_MODEL), const2, pipeline_mode=resident),
                  pl.BlockSpec((MLA_HEADS * MLA_V, D_MODEL), const2, pipeline_mode=resident),
                  pl.BlockSpec((SB_WIDTH, D_MODEL), const2, pipeline_mode=resident),
                  pl.BlockSpec((D_MODEL, D_MODEL), const2, pipeline_mode=resident),
                  pl.BlockSpec((1, D_MODEL), const2),
                  pl.BlockSpec((D_MODEL, LANES), const2),
                  pl.BlockSpec((1, LANES), const2)],
        out_specs=(pl.BlockSpec((tm, D_MODEL), row), pl.BlockSpec((tm, D_MODEL), row),
                   pl.BlockSpec((tm, LANES), row)),
        out_shape=(jax.ShapeDtypeStruct((t, D_MODEL), F32), jax.ShapeDtypeStruct((t, D_MODEL), BF16),
                   jax.ShapeDtypeStruct((t, LANES), F32)),
        compiler_params=_params("parallel"), name="merge",
    )(x2, yp, ym, ys, lw["g_mix"], lw["w_gate"], lw["w_br_pool"], lw["w_br_mla"], lw["w_br_sb"],
      lw["w_out"], lw["g_ffn"], lw["w_router"], lw["b_router"])


def _moe_kernel(xn_ref, comb_ref, x1_ref, wg_ref, wu_ref, wd_ref, o_ref):
    xn = xn_ref[...]
    comb = _route(comb_ref[...])
    group_w = EXPERTS_PER_GROUP * D_EXPERT
    y = x1_ref[...]
    for g in range(N_GROUPS):
        cols = slice(g * group_w, (g + 1) * group_w)
        a = _dot(xn, wg_ref[:, cols])
        h = (a * _sigmoid(a)) * _dot(xn, wu_ref[:, cols])
        scaled = []
        for j in range(EXPERTS_PER_GROUP):
            lane = ROUTER_EXPERT_LANE0 + g * EXPERTS_PER_GROUP + j
            scaled.append((h[:, j * D_EXPERT:(j + 1) * D_EXPERT] * comb[:, lane:lane + 1]).astype(BF16))
        y = y + _dot(jnp.concatenate(scaled, axis=1), wd_ref[cols, :])
    o_ref[...] = y


def _moe(xn2, comb, x1, lw):
    t = xn2.shape[0]
    tm = _tile(t, 512)
    row = lambda i: (i, 0)
    const2 = lambda i: (0, 0)
    hidden = N_EXPERTS * D_EXPERT
    resident = pl.Buffered(1)
    return pl.pallas_call(
        _moe_kernel, grid=(t // tm,),
        in_specs=[pl.BlockSpec((tm, D_MODEL), row), pl.BlockSpec((tm, LANES), row),
                  pl.BlockSpec((tm, D_MODEL), row),
                  pl.BlockSpec((D_MODEL, hidden), const2, pipeline_mode=resident),
                  pl.BlockSpec((D_MODEL, hidden), const2, pipeline_mode=resident),
                  pl.BlockSpec((hidden, D_MODEL), const2, pipeline_mode=resident)],
        out_specs=pl.BlockSpec((tm, D_MODEL), row),
        out_shape=jax.ShapeDtypeStruct((t, D_MODEL), F32),
        compiler_params=_params("parallel"), name="moe",
    )(xn2, comb, x1, lw["w_exp_gate"], lw["w_exp_up"], lw["w_exp_down"])


def _head_blocks(w, n_heads, width, place):
    k = w.shape[0]
    blocks = [place(h, w[:, h * width:(h + 1) * width]) for h in range(n_heads)]
    return jnp.concatenate(blocks, axis=1).reshape(k, n_heads * LANES)


def _pad_lanes(w, before, total=LANES):
    return jnp.pad(w, ((0, 0), (before, total - before - w.shape[1])))


def _swap_halves(w):
    half = w.shape[1] // 2
    return jnp.concatenate([-w[:, half:], w[:, :half]], axis=1)


def _experts_side_by_side(w):
    e, d, f = w.shape
    return jnp.transpose(w.astype(BF16), (1, 0, 2)).reshape(d, e * f)


def _prep_layer(p):
    w_in = p["w_in"].astype(BF16)
    w_uq = p["w_uq"].astype(BF16)
    w_ukv = p["w_ukv"].astype(BF16)
    w_kr = w_in[:, 640:672]
    w_cat = jnp.concatenate(
        [w_in[:, 0:640], w_in[:, 672:1440],
         _pad_lanes(w_kr, MLA_NOPE), _pad_lanes(_swap_halves(w_kr), MLA_NOPE)], axis=1)
    wq = _head_blocks(w_uq, MLA_HEADS, MLA_QK, lambda h, b: _pad_lanes(b, 0))
    wqs = _head_blocks(w_uq, MLA_HEADS, MLA_QK,
                       lambda h, b: _pad_lanes(_swap_halves(b[:, MLA_NOPE:]), MLA_NOPE))
    wk = _head_blocks(w_ukv, MLA_HEADS, MLA_NOPE + MLA_V,
                      lambda h, b: _pad_lanes(b[:, :MLA_NOPE], 0))
    wv = _head_blocks(w_ukv, MLA_HEADS, MLA_NOPE + MLA_V,
                      lambda h, b: _pad_lanes(b[:, MLA_NOPE:], (h % 2) * MLA_V))
    w_pool = jax.scipy.linalg.block_diag(*[p["w_pool_lin"][g] for g in range(4)])
    w_router = _pad_lanes(jnp.concatenate([p["w_router_group"], p["w_router_expert"]], axis=1), 0)
    b_router = _pad_lanes(jnp.concatenate([p["b_router_group"], p["b_router_expert"]])[None, :], 0)
    return {
        "g_mix": p["g_mix_norm"][None, :],
        "w_cat": w_cat.astype(BF16),
        "g_q_lora": p["g_q_lora"][None, :],
        "wq": wq.astype(BF16), "wqs": wqs.astype(BF16),
        "g_qk_q": _pad_lanes(p["g_qk_q"][None, :], 0),
        "g_kv": p["g_kv_lora"][None, :],
        "wk": wk.astype(BF16), "wv": wv.astype(BF16),
        "g_qk_k": _pad_lanes(p["g_qk_k"][None, :], 0),
        "mla_bound": ((MLA_QK ** 0.5) * LOG2_E * MLA_BOUND_SLACK
                      * jnp.max(jnp.abs(p["g_qk_q"])) * jnp.max(jnp.abs(p["g_qk_k"]))).reshape(1),
        "w_pool": w_pool.astype(BF16),
        "pool_scale": p["pool_scale"][None, :],
        "w_gate": w_in[:, 1440:].astype(BF16),
        "w_br_pool": p["w_br_pool"].astype(BF16),
        "w_br_mla": p["w_br_mla"].astype(BF16),
        "w_br_sb": p["w_br_sb"].astype(BF16),
        "w_out": p["w_out"].astype(BF16),
        "g_ffn": p["g_ffn_norm"][None, :],
        "w_router": w_router.astype(BF16),
        "b_router": b_router,
        "w_exp_gate": _experts_side_by_side(p["w_exp_gate"]),
        "w_exp_up": _experts_side_by_side(p["w_exp_up"]),
        "w_exp_down": p["w_exp_down"].astype(BF16).reshape(N_EXPERTS * D_EXPERT, D_MODEL),
    }


def _rope_table(pos):
    half = MLA_ROPE // 2
    inv = ROPE_THETA ** (-jnp.arange(half, dtype=F32) / half)
    ang = pos.astype(F32)[:, None] * inv[None, :]
    cos, sin = jnp.cos(ang), jnp.sin(ang)
    n = pos.shape[0]
    cos_t = jnp.concatenate([jnp.ones((n, MLA_NOPE), F32), cos, cos, jnp.zeros((n, LANES - MLA_QK), F32)], axis=1)
    sin_t = jnp.concatenate([jnp.zeros((n, MLA_NOPE), F32), sin, sin, jnp.zeros((n, LANES - MLA_QK), F32)], axis=1)
    return jnp.stack([cos_t, sin_t])


def _rows_into(rows, old, new, lane0=0, width=None):
    b, n_old, w_old = old.shape
    width = w_old if width is None else width
    old = jnp.pad(old.astype(new.dtype), ((0, 0), (0, 0), (lane0, width - lane0 - w_old)))
    new = new.reshape(b, -1, width)
    tail = jnp.zeros((b, rows - n_old - new.shape[1], width), new.dtype)
    return jnp.concatenate([old, new, tail], axis=1)


def _layer(x, cs, lw, hist, past):
    b, seq_len, _ = x.shape
    t = b * seq_len
    x2 = x.reshape(t, D_MODEL)
    up, q, lat, kr, qsb, ksb, vsb, ksbh, vsbh, kr_new, k_mla, v_mla = _inproj(x2, cs, lw, seq_len)
    if hist is None:
        lk = seq_len
        kv_len = seq_len
        ksb_all, vsb_all = ksbh, vsbh
        hist16 = jnp.zeros((b, POOL_HALO, POOL_WIDTH), F32)
    else:
        kv_len = past + seq_len
        lk = -(-kv_len // 256) * 256
        lat_all = _rows_into(lk, hist["lat"], lat).reshape(b * lk, MLA_KV_LORA)
        kr_all = _rows_into(lk, hist["kr"], kr, MLA_NOPE, LANES).reshape(b * lk, LANES)
        ksb_all = _rows_into(lk, hist["sbk"].reshape(b, past, SB_WIDTH), ksbh).reshape(b * lk, SB_WIDTH)
        vsb_all = _rows_into(lk, hist["sbv"].reshape(b, past, SB_WIDTH), vsbh).reshape(b * lk, SB_WIDTH)
        hist16 = jnp.pad(hist["pool"], ((0, 0), (POOL_HALO - POOL_HIST, 0), (0, 0)))
        k_mla, v_mla = _mla_keys(lat_all, kr_all, lw)
    y_mla = _mla_attn(lw["mla_bound"], q, k_mla, v_mla, b, seq_len, lk, past, kv_len)
    y_sb = _sb_attn(qsb, ksb_all, vsb_all, b, seq_len, lk, past)
    y_pool = _pool(up, hist16, lw, b, seq_len, past)
    x1, xn2, comb = _merge(x2, y_pool, y_mla, y_sb, lw)
    x_out = _moe(xn2, comb, x1, lw).reshape(b, seq_len, D_MODEL)
    up3 = up.reshape(b, seq_len, POOL_WIDTH)
    if seq_len >= POOL_HIST:
        pool_state = up3[:, seq_len - POOL_HIST:]
    else:
        pool_state = jnp.concatenate([hist16[:, 1:], up3], axis=1)[:, -POOL_HIST:]
    state = (lat.reshape(b, seq_len, MLA_KV_LORA),
             kr_new.reshape(b, seq_len, MLA_ROPE),
             ksb.reshape(b, seq_len, SB_HEADS, SB_HEAD_DIM),
             vsb.reshape(b, seq_len, SB_HEADS, SB_HEAD_DIM),
             pool_state)
    return x_out, state


def kernel(x_prompt, x_sample, cache_mla_latent, cache_mla_krope, cache_sb_k, cache_sb_v, state_pool,
           g_mix_norm, w_in, w_pool_lin, pool_scale, g_q_lora, w_uq, g_kv_lora, w_ukv, g_qk_q, g_qk_k,
           w_br_pool, w_br_mla, w_br_sb, w_out, g_ffn_norm, w_router_group, b_router_group,
           w_router_expert, b_router_expert, w_exp_gate, w_exp_up, w_exp_down):
    depth = w_in.shape[0]
    past = cache_mla_latent.shape[2]
    cs_p = _rope_table(jnp.arange(x_prompt.shape[1], dtype=jnp.int32))
    cs_s = _rope_table(past + jnp.arange(x_sample.shape[1], dtype=jnp.int32))
    xp, xs = x_prompt, x_sample
    new_p = [[] for _ in range(5)]
    new_s = [[] for _ in range(5)]
    for l in range(depth):
        lw = _prep_layer({
            "g_mix_norm": g_mix_norm[l], "w_in": w_in[l], "w_pool_lin": w_pool_lin[l],
            "pool_scale": pool_scale[l], "g_q_lora": g_q_lora[l], "w_uq": w_uq[l],
            "g_kv_lora": g_kv_lora[l], "w_ukv": w_ukv[l], "g_qk_q": g_qk_q[l], "g_qk_k": g_qk_k[l],
            "w_br_pool": w_br_pool[l], "w_br_mla": w_br_mla[l], "w_br_sb": w_br_sb[l],
            "w_out": w_out[l], "g_ffn_norm": g_ffn_norm[l],
            "w_router_group": w_router_group[l], "b_router_group": b_router_group[l],
            "w_router_expert": w_router_expert[l], "b_router_expert": b_router_expert[l],
            "w_exp_gate": w_exp_gate[l], "w_exp_up": w_exp_up[l], "w_exp_down": w_exp_down[l],
        })
        hist = {"lat": cache_mla_latent[l], "kr": cache_mla_krope[l], "sbk": cache_sb_k[l],
                "sbv": cache_sb_v[l], "pool": state_pool[l]}
        xp, st_p = _layer(xp, cs_p, lw, None, 0)
        xs, st_s = _layer(xs, cs_s, lw, hist, past)
        for i in range(5):
            new_p[i].append(st_p[i])
            new_s[i].append(st_s[i])
    return (xp, xs, *[jnp.stack(s) for s in new_p], *[jnp.stack(s) for s in new_s])
```
